```python
import jax, jax.numpy as jnp
from jax import lax
import numpy as np

D_MODEL = 1024
BATCH = 16
SEQ = 4096
DEPTH = 1

MIX_WIDTH = D_MODEL
SB_HEADS = 8
SB_HEAD_DIM = 64
SB_WIDTH = SB_HEADS * SB_HEAD_DIM
SB_SCALE = SB_HEAD_DIM ** -0.5
MLA_HEADS = 4
QK_NOPE = 128
QK_ROPE = 64
V_DIM = 128
Q_LORA = 256
KV_LORA = 128
MLA_WIDTH = MLA_HEADS * V_DIM
MLA_SCALE = (QK_NOPE + QK_ROPE) ** -0.5
ROPE_BASE = 10000.0
IN_COLS = 3 * SB_WIDTH + Q_LORA + KV_LORA + QK_ROPE
D_FF = 4 * D_MODEL
BLOCK_Q = 128
NORM_EPS = 1e-6

kernel_name = 'hymba_stickbreaking_mla_sqrelu_block'


def rmsnorm(x, g):
    xf = x.astype(jnp.float32)
    y = xf * lax.rsqrt(jnp.mean(jnp.square(xf), axis=-1, keepdims=True) + NORM_EPS)
    return (y * g.astype(jnp.float32)).astype(x.dtype)


def rope_tables(positions):
    half = QK_ROPE // 2
    inv_freq = ROPE_BASE ** (-jnp.arange(half, dtype=jnp.float32) / half)
    ang = positions.astype(jnp.float32)[..., None] * inv_freq
    return jnp.cos(ang), jnp.sin(ang)


def apply_rope(x, cos, sin):
    half = QK_ROPE // 2
    xf = x.astype(jnp.float32)
    x1, x2 = xf[..., :half], xf[..., half:]
    out = jnp.concatenate([x1 * cos - x2 * sin, x2 * cos + x1 * sin], axis=-1)
    return out.astype(x.dtype)


def stick_breaking_block(q_blk, k_pre, v_pre, t0):
    n_q, n_k = q_blk.shape[1], k_pre.shape[1]
    z = jnp.einsum('bqhd,bkhd->bhqk', q_blk, k_pre).astype(jnp.float32) * SB_SCALE
    t_idx = t0 + jnp.arange(n_q)
    s_idx = jnp.arange(n_k)
    strict = s_idx[None, :] < t_idx[:, None]
    sp = jnp.where(strict, jax.nn.softplus(z), 0.0)
    excl = lax.cumsum(sp, axis=3, reverse=True) - sp
    log_a = jax.nn.log_sigmoid(z) - excl
    a = jnp.where(strict, jnp.exp(log_a), 0.0)
    return jnp.einsum('bhqk,bkhd->bqhd', a.astype(v_pre.dtype), v_pre)


def mla_block(qn_blk, qr_blk, kn_pre, kr_pre, v_pre, t0):
    n_q, n_k = qn_blk.shape[1], kn_pre.shape[1]
    s = (jnp.einsum('bqhd,bkhd->bhqk', qn_blk, kn_pre)
         + jnp.einsum('bqhr,bkr->bhqk', qr_blk, kr_pre)).astype(jnp.float32) * MLA_SCALE
    t_idx = t0 + jnp.arange(n_q)
    s_idx = jnp.arange(n_k)
    causal = s_idx[None, :] <= t_idx[:, None]
    s = jnp.where(causal, s, jnp.finfo(jnp.float32).min)
    p = jax.nn.softmax(s, axis=-1)
    return jnp.einsum('bhqk,bkhd->bqhd', p.astype(v_pre.dtype), v_pre)


def setup_inputs(seed: int = 0) -> dict:
    key = jax.random.key(seed)
    ks = jax.random.split(key, 16)

    def w(k, shape, fan_in):
        return jax.random.normal(k, shape, jnp.float32) * fan_in ** -0.5

    def g(k, shape):
        return 1.0 + 0.01 * jax.random.normal(k, shape, jnp.float32)

    return {
        'x': jax.random.normal(ks[0], (BATCH, SEQ, D_MODEL), jnp.float32),
        'positions': jnp.broadcast_to(jnp.arange(SEQ, dtype=jnp.int32), (BATCH, SEQ)),
        'attn_norm_g': g(ks[1], (DEPTH, D_MODEL)),
        'w_in': w(ks[2], (DEPTH, D_MODEL, IN_COLS), D_MODEL),
        'q_a_norm_g': g(ks[3], (DEPTH, Q_LORA)),
        'w_q_b': w(ks[4], (DEPTH, Q_LORA, MLA_HEADS * (QK_NOPE + QK_ROPE)), Q_LORA),
        'kv_a_norm_g': g(ks[5], (DEPTH, KV_LORA)),
        'w_kv_b': w(ks[6], (DEPTH, KV_LORA, MLA_HEADS * (QK_NOPE + V_DIM)), KV_LORA),
        'sb_out_norm_g': g(ks[7], (DEPTH, SB_WIDTH)),
        'mla_out_norm_g': g(ks[8], (DEPTH, MLA_WIDTH)),
        'w_o': w(ks[9], (DEPTH, MIX_WIDTH, D_MODEL), MIX_WIDTH),
        'mlp_norm_g': g(ks[10], (DEPTH, D_MODEL)),
        'w_up': w(ks[11], (DEPTH, D_MODEL, D_FF), D_MODEL),
        'w_down': w(ks[12], (DEPTH, D_FF, D_MODEL), D_FF),
        'final_norm_g': g(ks[13], (D_MODEL,)),
    }


def reference(x, positions, attn_norm_g, w_in, q_a_norm_g, w_q_b, kv_a_norm_g, w_kv_b,
              sb_out_norm_g, mla_out_norm_g, w_o, mlp_norm_g, w_up, w_down, final_norm_g):
    bsz, seq = x.shape[0], x.shape[1]
    n_blocks = seq // BLOCK_Q
    cos, sin = rope_tables(positions)
    splits = [SB_WIDTH, 2 * SB_WIDTH, 3 * SB_WIDTH, 3 * SB_WIDTH + Q_LORA,
              3 * SB_WIDTH + Q_LORA + KV_LORA]
    h = x
    for l in range(DEPTH):
        u = rmsnorm(h, attn_norm_g[l])
        proj = jnp.einsum('bsd,de->bse', u, w_in[l])
        q_sb, k_sb, v_sb, c_q, c_kv, k_rope = jnp.split(proj, splits, axis=-1)
        q_sb = q_sb.reshape(bsz, seq, SB_HEADS, SB_HEAD_DIM)
        k_sb = k_sb.reshape(bsz, seq, SB_HEADS, SB_HEAD_DIM)
        v_sb = v_sb.reshape(bsz, seq, SB_HEADS, SB_HEAD_DIM)
        q_m = jnp.einsum('bsr,re->bse', rmsnorm(c_q, q_a_norm_g[l]), w_q_b[l])
        q_m = q_m.reshape(bsz, seq, MLA_HEADS, QK_NOPE + QK_ROPE)
        q_nope = q_m[..., :QK_NOPE]
        q_rope = apply_rope(q_m[..., QK_NOPE:], cos[:, :, None, :], sin[:, :, None, :])
        kv = jnp.einsum('bsr,re->bse', rmsnorm(c_kv, kv_a_norm_g[l]), w_kv_b[l])
        kv = kv.reshape(bsz, seq, MLA_HEADS, QK_NOPE + V_DIM)
        k_nope, v_m = kv[..., :QK_NOPE], kv[..., QK_NOPE:]
        k_rope = apply_rope(k_rope, cos, sin)
        sb_outs, mla_outs = [], []
        for i in range(n_blocks):
            t0 = i * BLOCK_Q
            t1 = t0 + BLOCK_Q
            sb_outs.append(stick_breaking_block(q_sb[:, t0:t1], k_sb[:, :t1], v_sb[:, :t1], t0))
            mla_outs.append(mla_block(q_nope[:, t0:t1], q_rope[:, t0:t1], k_nope[:, :t1],
                                      k_rope[:, :t1], v_m[:, :t1], t0))
        o_sb = jnp.concatenate(sb_outs, axis=1).reshape(bsz, seq, SB_WIDTH)
        o_mla = jnp.concatenate(mla_outs, axis=1).reshape(bsz, seq, MLA_WIDTH)
        mixed = jnp.concatenate([rmsnorm(o_sb, sb_out_norm_g[l]),
                                 rmsnorm(o_mla, mla_out_norm_g[l])], axis=-1)
        h = h + jnp.einsum('bse,ed->bsd', mixed, w_o[l])
        v = rmsnorm(h, mlp_norm_g[l])
        hid = jnp.square(jax.nn.relu(jnp.einsum('bsd,df->bsf', v, w_up[l])))
        h = h + jnp.einsum('bsf,fd->bsd', hid, w_down[l])
    return rmsnorm(h, final_norm_g)
```

```python
import functools

import jax
import jax.numpy as jnp
from jax import lax
from jax.experimental import pallas as pl
from jax.experimental.pallas import tpu as pltpu

SB_HEADS = 8
SB_HEAD_DIM = 64
SB_WIDTH = SB_HEADS * SB_HEAD_DIM
SB_SCALE = SB_HEAD_DIM ** -0.5
MLA_HEADS = 4
QK_NOPE = 128
QK_ROPE = 64
V_DIM = 128
Q_LORA = 256
KV_LORA = 128
MLA_WIDTH = MLA_HEADS * V_DIM
MLA_SCALE = (QK_NOPE + QK_ROPE) ** -0.5
ROPE_BASE = 10000.0
NORM_EPS = 1e-6

LANES = 128
MLA_QK_PAD = 256
PROJ_TILE = 512
SB_TILE = 256
MLA_TILE = 256
POST_TILE = 512
VMEM_LIMIT = 56 * 1024 * 1024
SB_DEAD_MASS = 110.0

_F32 = jnp.float32
_BF16 = jnp.bfloat16


def _dot(a, b):
    return jnp.dot(a, b, preferred_element_type=_F32)


def _dot_nt(a, b):
    return lax.dot_general(a, b, (((1,), (1,)), ((), ())), preferred_element_type=_F32)


def _rms(x, g):
    y = x * lax.rsqrt(jnp.mean(x * x, axis=-1, keepdims=True) + NORM_EPS)
    return y * g


def _proj_kernel(x_ref, pos_ref, invf_ref, sign_ref, g_attn_ref, w1_ref, wkt_ref, g_qa_ref,
                 g_kva_ref, w2_ref, wknt_ref, wvm_ref,
                 qsb_ref, ktsb_ref, vsb_ref, qcat_ref, ktm_ref, vm_ref):
    u = _rms(x_ref[0], g_attn_ref[...]).astype(_BF16)
    p1 = _dot(u, w1_ref[...])
    qsb_ref[0] = p1[:, :SB_WIDTH].astype(_BF16)
    vsb_ref[0] = p1[:, SB_WIDTH:2 * SB_WIDTH].astype(_BF16)
    ktsb_ref[0] = _dot_nt(wkt_ref[...], u).astype(_BF16)

    o = 2 * SB_WIDTH
    cqn = _rms(p1[:, o:o + Q_LORA], g_qa_ref[...]).astype(_BF16)
    o += Q_LORA
    ckvn = _rms(p1[:, o:o + KV_LORA], g_kva_ref[...]).astype(_BF16)
    o += KV_LORA
    kr = p1[:, o:o + LANES]
    kr_rot = p1[:, o + LANES:o + 2 * LANES]

    ang = pos_ref[0].astype(_F32) * invf_ref[...]
    cosm = jnp.cos(ang)
    sinm = jnp.sin(ang) * sign_ref[...]

    kr_roped = kr * cosm + kr_rot * sinm
    krt = kr_roped.T.astype(_BF16)

    p2 = _dot(cqn, w2_ref[...])
    knt = _dot_nt(wknt_ref[...], ckvn).astype(_BF16)
    hw = MLA_HEADS * LANES
    for h in range(MLA_HEADS):
        sl = slice(h * LANES, (h + 1) * LANES)
        qn = p2[:, sl] * MLA_SCALE
        qr = (p2[:, hw + h * LANES:hw + (h + 1) * LANES] * cosm
              + p2[:, 2 * hw + h * LANES:2 * hw + (h + 1) * LANES] * sinm) * MLA_SCALE
        qcat_ref[0, :, h * MLA_QK_PAD:h * MLA_QK_PAD + LANES] = qn.astype(_BF16)
        qcat_ref[0, :, h * MLA_QK_PAD + LANES:(h + 1) * MLA_QK_PAD] = qr.astype(_BF16)
        ktm_ref[0, h, :QK_NOPE, :] = knt[h * QK_NOPE:(h + 1) * QK_NOPE]
        ktm_ref[0, h, QK_NOPE:, :] = krt
    vm_ref[0] = _dot(ckvn, wvm_ref[...]).astype(_BF16)


def _full(shape):
    return pl.BlockSpec(shape, lambda *_: (0,) * len(shape))


def _proj(x, pos3, invf, sign, g_attn, w1, wkt, g_qa, g_kva, w2, wknt, wvm):
    bsz, seq, d = x.shape
    ts = PROJ_TILE
    grid = (bsz, seq // ts)
    row = lambda b, s: (b, s, 0)
    col = lambda b, s: (b, 0, s)
    out_shape = (
        jax.ShapeDtypeStruct((bsz, seq, SB_WIDTH), _BF16),
        jax.ShapeDtypeStruct((bsz, SB_WIDTH, seq), _BF16),
        jax.ShapeDtypeStruct((bsz, seq, SB_WIDTH), _BF16),
        jax.ShapeDtypeStruct((bsz, seq, MLA_HEADS * MLA_QK_PAD), _BF16),
        jax.ShapeDtypeStruct((bsz, MLA_HEADS, MLA_QK_PAD, seq), _BF16),
        jax.ShapeDtypeStruct((bsz, seq, MLA_WIDTH), _BF16),
    )
    return pl.pallas_call(
        _proj_kernel,
        grid=grid,
        in_specs=[
            pl.BlockSpec((1, ts, d), row),
            pl.BlockSpec((1, ts, 1), row),
            _full(invf.shape), _full(sign.shape), _full(g_attn.shape), _full(w1.shape),
            _full(wkt.shape), _full(g_qa.shape), _full(g_kva.shape), _full(w2.shape),
            _full(wknt.shape), _full(wvm.shape),
        ],
        out_specs=(
            pl.BlockSpec((1, ts, SB_WIDTH), row),
            pl.BlockSpec((1, SB_WIDTH, ts), col),
            pl.BlockSpec((1, ts, SB_WIDTH), row),
            pl.BlockSpec((1, ts, MLA_HEADS * MLA_QK_PAD), row),
            pl.BlockSpec((1, MLA_HEADS, MLA_QK_PAD, ts), lambda b, s: (b, 0, 0, s)),
            pl.BlockSpec((1, ts, MLA_WIDTH), row),
        ),
        out_shape=out_shape,
        compiler_params=pltpu.CompilerParams(
            dimension_semantics=("arbitrary", "arbitrary"), vmem_limit_bytes=VMEM_LIMIT),
        name="proj",
    )(x, pos3, invf, sign, g_attn, w1, wkt, g_qa, g_kva, w2, wknt, wvm)


def _sb_kernel(q_ref, kt_ref, v_ref, tri_ref, o_ref, acc_ref, mass_ref):
    t = SB_TILE
    i = pl.program_id(2)
    q = q_ref[0]
    lane = lax.broadcasted_iota(jnp.int32, (t, LANES), 1)
    first = lane < SB_HEAD_DIM
    zero = jnp.zeros_like(q)
    qs = (jnp.where(first, q, zero), jnp.where(first, zero, q))
    tri = tri_ref[...]
    acc_ref[...] = jnp.zeros_like(acc_ref)
    mass_ref[...] = jnp.zeros_like(mass_ref)

    def tile(j, masked):
        start = pl.multiple_of(j * t, t)
        kt = kt_ref[0, :, pl.ds(start, t)]
        v = v_ref[0, pl.ds(start, t), :]
        if masked:
            r = lax.broadcasted_iota(jnp.int32, (t, t), 0)
            c = lax.broadcasted_iota(jnp.int32, (t, t), 1)
            strict = c < r
        for hh in range(2):
            z = _dot(qs[hh], kt)
            sp = jnp.maximum(z, 0.0) + jnp.log(1.0 + jnp.exp(-jnp.abs(z)))
            if masked:
                sp = jnp.where(strict, sp, 0.0)
            excl = _dot(sp.astype(_BF16), tri)
            mass = mass_ref[hh]
            a = jnp.exp(z - sp - excl - mass)
            if masked:
                a = jnp.where(strict, a, 0.0)
            acc_ref[hh] += _dot(a.astype(_BF16), v)
            mass_ref[hh] = mass + (excl[:, :1] + sp[:, :1])

    def min_mass():
        return jnp.min(jnp.minimum(mass_ref[0], mass_ref[1]))

    tile(i, True)

    def cond(carry):
        j, mm = carry
        return jnp.logical_and(j >= 0, mm < SB_DEAD_MASS)

    def body(carry):
        j, _ = carry
        tile(j, False)
        return j - 1, min_mass()

    lax.while_loop(cond, body, (i - 1, min_mass()))
    o_ref[0] = jnp.where(first, acc_ref[0], acc_ref[1])


def _sb_attention(qsb, ktsb, vsb, tri):
    bsz, seq, _ = qsb.shape
    t = SB_TILE
    grid = (bsz, SB_WIDTH // LANES, seq // t)
    return pl.pallas_call(
        _sb_kernel,
        grid=grid,
        in_specs=[
            pl.BlockSpec((1, t, LANES), lambda b, p, i: (b, i, p)),
            pl.BlockSpec((1, LANES, seq), lambda b, p, i: (b, p, 0)),
            pl.BlockSpec((1, seq, LANES), lambda b, p, i: (b, 0, p)),
            pl.BlockSpec((t, t), lambda b, p, i: (0, 0)),
        ],
        out_specs=pl.BlockSpec((1, t, LANES), lambda b, p, i: (b, i, p)),
        out_shape=jax.ShapeDtypeStruct((bsz, seq, SB_WIDTH), _F32),
        scratch_shapes=[pltpu.VMEM((2, t, LANES), _F32), pltpu.VMEM((2, t, 1), _F32)],
        compiler_params=pltpu.CompilerParams(
            dimension_semantics=("arbitrary", "arbitrary", "arbitrary"),
            vmem_limit_bytes=VMEM_LIMIT),
        name="sb_attn",
    )(qsb, ktsb, vsb, tri)


def _mla_kernel(q_ref, kt_ref, v_ref, o_ref, acc_ref, m_ref, l_ref):
    t = MLA_TILE
    i = pl.program_id(2)
    q = q_ref[0]
    acc_ref[...] = jnp.zeros_like(acc_ref)
    l_ref[...] = jnp.zeros_like(l_ref)
    m_ref[...] = jnp.full_like(m_ref, jnp.finfo(_F32).min)

    def tile(j, masked):
        start = pl.multiple_of(j * t, t)
        kt = kt_ref[0, 0, :, pl.ds(start, t)]
        v = v_ref[0, pl.ds(start, t), :]
        s = _dot(q, kt)
        if masked:
            r = lax.broadcasted_iota(jnp.int32, (t, t), 0)
            c = lax.broadcasted_iota(jnp.int32, (t, t), 1)
            s = jnp.where(c <= r, s, jnp.finfo(_F32).min)
        m_old = m_ref[...]
        m_new = jnp.maximum(m_old, jnp.max(s, axis=1, keepdims=True))
        p = jnp.exp(s - m_new)
        alpha = jnp.exp(m_old - m_new)
        l_ref[...] = alpha * l_ref[...] + jnp.sum(p, axis=1, keepdims=True)
        acc_ref[...] = alpha * acc_ref[...] + _dot(p.astype(_BF16), v)
        m_ref[...] = m_new

    def body(j, carry):
        tile(j, False)
        return carry

    lax.fori_loop(0, i, body, 0)
    tile(i, True)
    o_ref[0] = acc_ref[...] / l_ref[...]


def _mla_attention(qcat, ktm, vm):
    bsz, seq, _ = qcat.shape
    t = MLA_TILE
    grid = (bsz, MLA_HEADS, seq // t)
    return pl.pallas_call(
        _mla_kernel,
        grid=grid,
        in_specs=[
            pl.BlockSpec((1, t, MLA_QK_PAD), lambda b, h, i: (b, i, h)),
            pl.BlockSpec((1, 1, MLA_QK_PAD, seq), lambda b, h, i: (b, h, 0, 0)),
            pl.BlockSpec((1, seq, V_DIM), lambda b, h, i: (b, 0, h)),
        ],
        out_specs=pl.BlockSpec((1, t, V_DIM), lambda b, h, i: (b, i, h)),
        out_shape=jax.ShapeDtypeStruct((bsz, seq, MLA_WIDTH), _F32),
        scratch_shapes=[pltpu.VMEM((t, V_DIM), _F32), pltpu.VMEM((t, 1), _F32),
                        pltpu.VMEM((t, 1), _F32)],
        compiler_params=pltpu.CompilerParams(
            dimension_semantics=("arbitrary", "arbitrary", "arbitrary"),
            vmem_limit_bytes=VMEM_LIMIT),
        name="mla_attn",
    )(qcat, ktm, vm)


def _post_kernel(x_ref, osb_ref, omla_ref, g_sb_ref, g_mla_ref, wo_ref, g_mlp_ref, wup_ref,
                 wdown_ref, g_out_ref, out_ref, *, apply_out_norm):
    ms = _rms(osb_ref[...], g_sb_ref[...]).astype(_BF16)
    mm = _rms(omla_ref[...], g_mla_ref[...]).astype(_BF16)
    h1 = x_ref[...] + _dot(jnp.concatenate([ms, mm], axis=1), wo_ref[...])
    v = _rms(h1, g_mlp_ref[...]).astype(_BF16)
    hid = jnp.square(jnp.maximum(_dot(v, wup_ref[...]), 0.0)).astype(_BF16)
    h2 = h1 + _dot(hid, wdown_ref[...])
    out_ref[...] = _rms(h2, g_out_ref[...]) if apply_out_norm else h2


def _post(x2, osb2, omla2, g_sb, g_mla, wo, g_mlp, wup, wdown, g_out, apply_out_norm):
    n, d = x2.shape
    tm = POST_TILE
    once = pl.Buffered(1)
    rows = lambda w: pl.BlockSpec((tm, w), lambda r: (r, 0))
    const = lambda a: pl.BlockSpec(a.shape, lambda r: (0, 0), pipeline_mode=once)
    return pl.pallas_call(
        functools.partial(_post_kernel, apply_out_norm=apply_out_norm),
        grid=(n // tm,),
        in_specs=[rows(d), rows(SB_WIDTH), rows(MLA_WIDTH), const(g_sb), const(g_mla), const(wo),
                  const(g_mlp), const(wup), const(wdown), const(g_out)],
        out_specs=rows(d),
        out_shape=jax.ShapeDtypeStruct((n, d), _F32),
        compiler_params=pltpu.CompilerParams(
            dimension_semantics=("arbitrary",), vmem_limit_bytes=VMEM_LIMIT),
        name="post",
    )(x2, osb2, omla2, g_sb, g_mla, wo, g_mlp, wup, wdown, g_out)


def _swap_halves(w):
    half = QK_ROPE // 2
    return jnp.concatenate([w[..., half:], w[..., :half]], axis=-1)


def _layer_weights(w_in, w_q_b, w_kv_b):
    d = w_in.shape[0]
    s1, s2, s3 = SB_WIDTH, 2 * SB_WIDTH, 3 * SB_WIDTH
    s4, s5 = s3 + Q_LORA, s3 + Q_LORA + KV_LORA
    wkr = w_in[:, s5:]
    pad = jnp.zeros((d, LANES - QK_ROPE), w_in.dtype)
    w1 = jnp.concatenate([w_in[:, :s1] * SB_SCALE, w_in[:, s2:s3], w_in[:, s3:s4], w_in[:, s4:s5],
                          wkr, pad, _swap_halves(wkr), pad], axis=1).astype(_BF16)
    wkt = w_in[:, s1:s2].T.astype(_BF16)
    wqb = w_q_b.reshape(Q_LORA, MLA_HEADS, QK_NOPE + QK_ROPE)
    qn = wqb[:, :, :QK_NOPE].reshape(Q_LORA, MLA_HEADS * QK_NOPE)
    qr = wqb[:, :, QK_NOPE:]
    qpad = jnp.zeros((Q_LORA, MLA_HEADS, LANES - QK_ROPE), w_q_b.dtype)
    qr_pad = jnp.concatenate([qr, qpad], axis=-1).reshape(Q_LORA, MLA_HEADS * LANES)
    qrr_pad = jnp.concatenate([_swap_halves(qr), qpad], axis=-1).reshape(Q_LORA, MLA_HEADS * LANES)
    w2 = jnp.concatenate([qn, qr_pad, qrr_pad], axis=1).astype(_BF16)
    wkvb = w_kv_b.reshape(KV_LORA, MLA_HEADS, QK_NOPE + V_DIM)
    wknt = wkvb[:, :, :QK_NOPE].reshape(KV_LORA, MLA_HEADS * QK_NOPE).T.astype(_BF16)
    wvm = wkvb[:, :, QK_NOPE:].reshape(KV_LORA, MLA_HEADS * V_DIM).astype(_BF16)
    return w1, wkt, w2, wknt, wvm


def kernel(x, positions, attn_norm_g, w_in, q_a_norm_g, w_q_b, kv_a_norm_g, w_kv_b, sb_out_norm_g,
           mla_out_norm_g, w_o, mlp_norm_g, w_up, w_down, final_norm_g):
    bsz, seq, d = x.shape
    depth = w_in.shape[0]
    assert seq % PROJ_TILE == 0 and seq % SB_TILE == 0 and seq % MLA_TILE == 0
    assert (bsz * seq) % POST_TILE == 0

    half = QK_ROPE // 2
    inv_freq = ROPE_BASE ** (-jnp.arange(half, dtype=_F32) / half)
    invf = jnp.tile(inv_freq, LANES // half)[None, :]
    sign = jnp.tile(jnp.concatenate([-jnp.ones((half,), _F32), jnp.ones((half,), _F32)]),
                    LANES // QK_ROPE)[None, :]
    pos3 = positions[:, :, None]
    r = lax.broadcasted_iota(jnp.int32, (SB_TILE, SB_TILE), 0)
    c = lax.broadcasted_iota(jnp.int32, (SB_TILE, SB_TILE), 1)
    tri = (r > c).astype(_BF16)

    row = lambda g: g[None, :]
    h = x
    for l in range(depth):
        w1, wkt, w2, wknt, wvm = _layer_weights(w_in[l], w_q_b[l], w_kv_b[l])
        qsb, ktsb, vsb, qcat, ktm, vm = _proj(
            h, pos3, invf, sign, row(attn_norm_g[l]), w1, wkt, row(q_a_norm_g[l]),
            row(kv_a_norm_g[l]), w2, wknt, wvm)
        osb = _sb_attention(qsb, ktsb, vsb, tri)
        omla = _mla_attention(qcat, ktm, vm)
        last = l == depth - 1
        h = _post(h.reshape(bsz * seq, d), osb.reshape(bsz * seq, SB_WIDTH),
                  omla.reshape(bsz * seq, MLA_WIDTH), row(sb_out_norm_g[l]),
                  row(mla_out_norm_g[l]), w_o[l].astype(_BF16), row(mlp_norm_g[l]),
                  w_up[l].astype(_BF16), w_down[l].astype(_BF16), row(final_norm_g),
                  last).reshape(bsz, seq, d)
    return h
```

```python
import functools

import jax
import jax.numpy as jnp
from jax import lax
from jax.experimental import pallas as pl
from jax.experimental.pallas import tpu as pltpu

SB_HEADS = 8
SB_HEAD_DIM = 64
SB_WIDTH = SB_HEADS * SB_HEAD_DIM
SB_SCALE = SB_HEAD_DIM ** -0.5
MLA_HEADS = 4
QK_NOPE = 128
QK_ROPE = 64
V_DIM = 128
Q_LORA = 256
KV_LORA = 128
MLA_WIDTH = MLA_HEADS * V_DIM
MLA_SCALE = (QK_NOPE + QK_ROPE) ** -0.5
LOG2_E = 1.4426950408889634
ROPE_BASE = 10000.0
NORM_EPS = 1e-6

LANES = 128
MLA_QK_PAD = 256
V_EXT = V_DIM + 16
PROJ_TILE = 512
SB_TILE = 256
MLA_TILE = 512
POST_TILE = 512
VMEM_LIMIT = 56 * 1024 * 1024
SB_DEAD_MASS = 110.0

_F32 = jnp.float32
_BF16 = jnp.bfloat16


def _dot(a, b):
    return jnp.dot(a, b, preferred_element_type=_F32)


def _dot_nt(a, b):
    return lax.dot_general(a, b, (((1,), (1,)), ((), ())), preferred_element_type=_F32)


def _rms(x, g):
    y = x * lax.rsqrt(jnp.mean(x * x, axis=-1, keepdims=True) + NORM_EPS)
    return y * g


def _proj_kernel(x_ref, pos_ref, invf_ref, sign_ref, g_attn_ref, w1_ref, wkt_ref, g_qa_ref,
                 g_kva_ref, w2_ref, wkn_ref, wvmt_ref,
                 qsb_ref, ktsb_ref, vsb_ref, qcat_ref, kcat_ref, vmt_ref):
    u = _rms(x_ref[0], g_attn_ref[...]).astype(_BF16)
    p1 = _dot(u, w1_ref[...])
    qsb_ref[0] = p1[:, :SB_WIDTH].astype(_BF16)
    vsb_ref[0] = p1[:, SB_WIDTH:2 * SB_WIDTH].astype(_BF16)
    ktsb_ref[0] = _dot_nt(wkt_ref[...], u).astype(_BF16)

    o = 2 * SB_WIDTH
    cqn = _rms(p1[:, o:o + Q_LORA], g_qa_ref[...]).astype(_BF16)
    o += Q_LORA
    ckvn = _rms(p1[:, o:o + KV_LORA], g_kva_ref[...]).astype(_BF16)
    o += KV_LORA
    kr = p1[:, o:o + LANES]
    kr_rot = p1[:, o + LANES:o + 2 * LANES]

    ang = pos_ref[0].astype(_F32) * invf_ref[...]
    cosm = jnp.cos(ang)
    sinm = jnp.sin(ang) * sign_ref[...]

    kr_roped = (kr * cosm + kr_rot * sinm).astype(_BF16)

    p2 = _dot(cqn, w2_ref[...])
    kn = _dot(ckvn, wkn_ref[...]).astype(_BF16)
    hw = MLA_HEADS * LANES
    for h in range(MLA_HEADS):
        sl = slice(h * LANES, (h + 1) * LANES)
        lo = slice(h * MLA_QK_PAD, h * MLA_QK_PAD + LANES)
        hi = slice(h * MLA_QK_PAD + LANES, (h + 1) * MLA_QK_PAD)
        qn = p2[:, sl] * (MLA_SCALE * LOG2_E)
        qr = (p2[:, hw + h * LANES:hw + (h + 1) * LANES] * cosm
              + p2[:, 2 * hw + h * LANES:2 * hw + (h + 1) * LANES] * sinm) * (MLA_SCALE * LOG2_E)
        qcat_ref[0, :, lo] = qn.astype(_BF16)
        qcat_ref[0, :, hi] = qr.astype(_BF16)
        kcat_ref[0, :, lo] = kn[:, sl]
        kcat_ref[0, :, hi] = kr_roped
    vt = _dot_nt(wvmt_ref[...], ckvn).astype(_BF16)
    ts = vt.shape[1]
    ones_row = (lax.broadcasted_iota(jnp.int32, (V_EXT - V_DIM, ts), 0) == 0).astype(_BF16)
    for h in range(MLA_HEADS):
        vmt_ref[0, h, :V_DIM, :] = vt[h * V_DIM:(h + 1) * V_DIM]
        vmt_ref[0, h, V_DIM:, :] = ones_row


def _full(shape):
    return pl.BlockSpec(shape, lambda *_: (0,) * len(shape))


def _proj(x, pos3, invf, sign, g_attn, w1, wkt, g_qa, g_kva, w2, wkn, wvmt):
    bsz, seq, d = x.shape
    ts = PROJ_TILE
    grid = (bsz, seq // ts)
    row = lambda b, s: (b, s, 0)
    col = lambda b, s: (b, 0, s)
    out_shape = (
        jax.ShapeDtypeStruct((bsz, seq, SB_WIDTH), _BF16),
        jax.ShapeDtypeStruct((bsz, SB_WIDTH, seq), _BF16),
        jax.ShapeDtypeStruct((bsz, seq, SB_WIDTH), _BF16),
        jax.ShapeDtypeStruct((bsz, seq, MLA_HEADS * MLA_QK_PAD), _BF16),
        jax.ShapeDtypeStruct((bsz, seq, MLA_HEADS * MLA_QK_PAD), _BF16),
        jax.ShapeDtypeStruct((bsz, MLA_HEADS, V_EXT, seq), _BF16),
    )
    return pl.pallas_call(
        _proj_kernel,
        grid=grid,
        in_specs=[
            pl.BlockSpec((1, ts, d), row),
            pl.BlockSpec((1, ts, 1), row),
            _full(invf.shape), _full(sign.shape), _full(g_attn.shape), _full(w1.shape),
            _full(wkt.shape), _full(g_qa.shape), _full(g_kva.shape), _full(w2.shape),
            _full(wkn.shape), _full(wvmt.shape),
        ],
        out_specs=(
            pl.BlockSpec((1, ts, SB_WIDTH), row),
            pl.BlockSpec((1, SB_WIDTH, ts), col),
            pl.BlockSpec((1, ts, SB_WIDTH), row),
            pl.BlockSpec((1, ts, MLA_HEADS * MLA_QK_PAD), row),
            pl.BlockSpec((1, ts, MLA_HEADS * MLA_QK_PAD), row),
            pl.BlockSpec((1, MLA_HEADS, V_EXT, ts), lambda b, s: (b, 0, 0, s)),
        ),
        out_shape=out_shape,
        compiler_params=pltpu.CompilerParams(
            dimension_semantics=("arbitrary", "arbitrary"), vmem_limit_bytes=VMEM_LIMIT),
        name="proj",
    )(x, pos3, invf, sign, g_attn, w1, wkt, g_qa, g_kva, w2, wkn, wvmt)


def _sb_kernel(q_ref, kt_ref, v_ref, tri_ref, o_ref, acc_ref, mass_ref):
    t = SB_TILE
    i = pl.program_id(2)
    q = q_ref[0]
    lane = lax.broadcasted_iota(jnp.int32, (t, LANES), 1)
    first = lane < SB_HEAD_DIM
    zero = jnp.zeros_like(q)
    qs = (jnp.where(first, q, zero), jnp.where(first, zero, q))
    tri = tri_ref[...]
    acc_ref[...] = jnp.zeros_like(acc_ref)
    mass_ref[...] = jnp.zeros_like(mass_ref)

    def tile(j, masked):
        start = pl.multiple_of(j * t, t)
        kt = kt_ref[0, :, pl.ds(start, t)]
        v = v_ref[0, pl.ds(start, t), :]
        if masked:
            r = lax.broadcasted_iota(jnp.int32, (t, t), 0)
            c = lax.broadcasted_iota(jnp.int32, (t, t), 1)
            strict = c < r
        for hh in range(2):
            z = _dot(qs[hh], kt)
            sp = jnp.maximum(z, 0.0) + jnp.log(1.0 + jnp.exp(-jnp.abs(z)))
            if masked:
                sp = jnp.where(strict, sp, 0.0)
            excl = _dot(sp.astype(_BF16), tri)
            mass = mass_ref[hh]
            a = jnp.exp(z - sp - excl - mass)
            if masked:
                a = jnp.where(strict, a, 0.0)
            acc_ref[hh] += _dot(a.astype(_BF16), v)
            mass_ref[hh] = mass + (excl[:, :1] + sp[:, :1])

    def min_mass():
        return jnp.min(jnp.minimum(mass_ref[0], mass_ref[1]))

    tile(i, True)

    def cond(carry):
        j, mm = carry
        return jnp.logical_and(j >= 0, mm < SB_DEAD_MASS)

    def body(carry):
        j, _ = carry
        tile(j, False)
        return j - 1, min_mass()

    lax.while_loop(cond, body, (i - 1, min_mass()))
    o_ref[0] = jnp.where(first, acc_ref[0], acc_ref[1])


def _sb_attention(qsb, ktsb, vsb, tri):
    bsz, seq, _ = qsb.shape
    t = SB_TILE
    grid = (bsz, SB_WIDTH // LANES, seq // t)
    return pl.pallas_call(
        _sb_kernel,
        grid=grid,
        in_specs=[
            pl.BlockSpec((1, t, LANES), lambda b, p, i: (b, i, p)),
            pl.BlockSpec((1, LANES, seq), lambda b, p, i: (b, p, 0)),
            pl.BlockSpec((1, seq, LANES), lambda b, p, i: (b, 0, p)),
            pl.BlockSpec((t, t), lambda b, p, i: (0, 0)),
        ],
        out_specs=pl.BlockSpec((1, t, LANES), lambda b, p, i: (b, i, p)),
        out_shape=jax.ShapeDtypeStruct((bsz, seq, SB_WIDTH), _F32),
        scratch_shapes=[pltpu.VMEM((2, t, LANES), _F32), pltpu.VMEM((2, t, 1), _F32)],
        compiler_params=pltpu.CompilerParams(
            dimension_semantics=("arbitrary", "arbitrary", "arbitrary"),
            vmem_limit_bytes=VMEM_LIMIT),
        name="sb_attn",
    )(qsb, ktsb, vsb, tri)


def _mla_kernel(q_ref, k_ref, vt_ref, o_ref, acc_ref, m_ref, sa_ref, sb_ref):
    t = MLA_TILE
    i = pl.program_id(2)
    qt = q_ref[0].astype(_F32).T.astype(_BF16)
    acc_ref[...] = jnp.zeros_like(acc_ref)
    m_ref[...] = jnp.full_like(m_ref, jnp.finfo(_F32).min)

    def scores_into(dst_ref, j):
        start = pl.multiple_of(j * t, t)
        dst_ref[...] = _dot(k_ref[0, pl.ds(start, t), :], qt)

    def update(src_ref, j, masked):
        start = pl.multiple_of(j * t, t)
        vt = vt_ref[0, 0, :, pl.ds(start, t)]
        s = src_ref[...]
        if masked:
            key = lax.broadcasted_iota(jnp.int32, (t, t), 0)
            qry = lax.broadcasted_iota(jnp.int32, (t, t), 1)
            s = jnp.where(key <= qry, s, jnp.finfo(_F32).min)
        m_old = m_ref[...]
        m_new = jnp.maximum(m_old, jnp.max(s, axis=0, keepdims=True))
        p = jnp.exp2(s - m_new)
        alpha = jnp.exp2(m_old - m_new)
        acc_ref[...] = alpha * acc_ref[...] + _dot(vt, p.astype(_BF16))
        m_ref[...] = m_new

    scores_into(sa_ref, 0)

    def pair(jj, carry):
        j = 2 * jj
        scores_into(sb_ref, j + 1)
        update(sa_ref, j, False)
        scores_into(sa_ref, j + 2)
        update(sb_ref, j + 1, False)
        return carry

    lax.fori_loop(0, i // 2, pair, 0)

    @pl.when(i % 2 == 1)
    def _():
        scores_into(sb_ref, i)
        update(sa_ref, i - 1, False)
        update(sb_ref, i, True)

    @pl.when(i % 2 == 0)
    def _():
        update(sa_ref, i, True)

    o_ref[0] = (acc_ref[:V_DIM, :] / acc_ref[V_DIM:V_DIM + 1, :]).T


def _mla_attention(qcat, kcat, vmt):
    bsz, seq, _ = qcat.shape
    t = MLA_TILE
    grid = (bsz, MLA_HEADS, seq // t)
    return pl.pallas_call(
        _mla_kernel,
        grid=grid,
        in_specs=[
            pl.BlockSpec((1, t, MLA_QK_PAD), lambda b, h, i: (b, i, h)),
            pl.BlockSpec((1, seq, MLA_QK_PAD), lambda b, h, i: (b, 0, h)),
            pl.BlockSpec((1, 1, V_EXT, seq), lambda b, h, i: (b, h, 0, 0)),
        ],
        out_specs=pl.BlockSpec((1, t, V_DIM), lambda b, h, i: (b, i, h)),
        out_shape=jax.ShapeDtypeStruct((bsz, seq, MLA_WIDTH), _F32),
        scratch_shapes=[pltpu.VMEM((V_EXT, t), _F32), pltpu.VMEM((1, t), _F32),
                        pltpu.VMEM((t, t), _F32), pltpu.VMEM((t, t), _F32)],
        compiler_params=pltpu.CompilerParams(
            dimension_semantics=("arbitrary", "arbitrary", "arbitrary"),
            vmem_limit_bytes=VMEM_LIMIT),
        name="mla_attn",
    )(qcat, kcat, vmt)


def _post_kernel(x_ref, osb_ref, omla_ref, g_sb_ref, g_mla_ref, wo_ref, g_mlp_ref, wup_ref,
                 wdown_ref, g_out_ref, out_ref, *, apply_out_norm):
    ms = _rms(osb_ref[...], g_sb_ref[...]).astype(_BF16)
    mm = _rms(omla_ref[...], g_mla_ref[...]).astype(_BF16)
    h1 = x_ref[...] + _dot(jnp.concatenate([ms, mm], axis=1), wo_ref[...])
    v = _rms(h1, g_mlp_ref[...]).astype(_BF16)
    hid = jnp.square(jnp.maximum(_dot(v, wup_ref[...]), 0.0)).astype(_BF16)
    h2 = h1 + _dot(hid, wdown_ref[...])
    out_ref[...] = _rms(h2, g_out_ref[...]) if apply_out_norm else h2


def _post(x2, osb2, omla2, g_sb, g_mla, wo, g_mlp, wup, wdown, g_out, apply_out_norm):
    n, d = x2.shape
    tm = POST_TILE
    once = pl.Buffered(1)
    rows = lambda w: pl.BlockSpec((tm, w), lambda r: (r, 0))
    const = lambda a: pl.BlockSpec(a.shape, lambda r: (0, 0), pipeline_mode=once)
    return pl.pallas_call(
        functools.partial(_post_kernel, apply_out_norm=apply_out_norm),
        grid=(n // tm,),
        in_specs=[rows(d), rows(SB_WIDTH), rows(MLA_WIDTH), const(g_sb), const(g_mla), const(wo),
                  const(g_mlp), const(wup), const(wdown), const(g_out)],
        out_specs=rows(d),
        out_shape=jax.ShapeDtypeStruct((n, d), _F32),
        compiler_params=pltpu.CompilerParams(
            dimension_semantics=("arbitrary",), vmem_limit_bytes=VMEM_LIMIT),
        name="post",
    )(x2, osb2, omla2, g_sb, g_mla, wo, g_mlp, wup, wdown, g_out)


def _swap_halves(w):
    half = QK_ROPE // 2
    return jnp.concatenate([w[..., half:], w[..., :half]], axis=-1)


def _layer_weights(w_in, w_q_b, w_kv_b):
    d = w_in.shape[0]
    s1, s2, s3 = SB_WIDTH, 2 * SB_WIDTH, 3 * SB_WIDTH
    s4, s5 = s3 + Q_LORA, s3 + Q_LORA + KV_LORA
    wkr = w_in[:, s5:]
    pad = jnp.zeros((d, LANES - QK_ROPE), w_in.dtype)
    w1 = jnp.concatenate([w_in[:, :s1] * SB_SCALE, w_in[:, s2:s3], w_in[:, s3:s4], w_in[:, s4:s5],
                          wkr, pad, _swap_halves(wkr), pad], axis=1).astype(_BF16)
    wkt = w_in[:, s1:s2].T.astype(_BF16)
    wqb = w_q_b.reshape(Q_LORA, MLA_HEADS, QK_NOPE + QK_ROPE)
    qn = wqb[:, :, :QK_NOPE].reshape(Q_LORA, MLA_HEADS * QK_NOPE)
    qr = wqb[:, :, QK_NOPE:]
    qpad = jnp.zeros((Q_LORA, MLA_HEADS, LANES - QK_ROPE), w_q_b.dtype)
    qr_pad = jnp.concatenate([qr, qpad], axis=-1).reshape(Q_LORA, MLA_HEADS * LANES)
    qrr_pad = jnp.concatenate([_swap_halves(qr), qpad], axis=-1).reshape(Q_LORA, MLA_HEADS * LANES)
    w2 = jnp.concatenate([qn, qr_pad, qrr_pad], axis=1).astype(_BF16)
    wkvb = w_kv_b.reshape(KV_LORA, MLA_HEADS, QK_NOPE + V_DIM)
    wkn = wkvb[:, :, :QK_NOPE].reshape(KV_LORA, MLA_HEADS * QK_NOPE).astype(_BF16)
    wvmt = wkvb[:, :, QK_NOPE:].reshape(KV_LORA, MLA_HEADS * V_DIM).T.astype(_BF16)
    return w1, wkt, w2, wkn, wvmt


def kernel(x, positions, attn_norm_g, w_in, q_a_norm_g, w_q_b, kv_a_norm_g, w_kv_b, sb_out_norm_g,
           mla_out_norm_g, w_o, mlp_norm_g, w_up, w_down, final_norm_g):
    bsz, seq, d = x.shape
    depth = w_in.shape[0]
    assert seq % PROJ_TILE == 0 and seq % SB_TILE == 0 and seq % MLA_TILE == 0
    assert (bsz * seq) % POST_TILE == 0

    half = QK_ROPE // 2
    inv_freq = ROPE_BASE ** (-jnp.arange(half, dtype=_F32) / half)
    invf = jnp.tile(inv_freq, LANES // half)[None, :]
    sign = jnp.tile(jnp.concatenate([-jnp.ones((half,), _F32), jnp.ones((half,), _F32)]),
                    LANES // QK_ROPE)[None, :]
    pos3 = positions[:, :, None]
    r = lax.broadcasted_iota(jnp.int32, (SB_TILE, SB_TILE), 0)
    c = lax.broadcasted_iota(jnp.int32, (SB_TILE, SB_TILE), 1)
    tri = (r > c).astype(_BF16)

    row = lambda g: g[None, :]
    h = x
    for l in range(depth):
        w1, wkt, w2, wkn, wvmt = _layer_weights(w_in[l], w_q_b[l], w_kv_b[l])
        qsb, ktsb, vsb, qcat, kcat, vmt = _proj(
            h, pos3, invf, sign, row(attn_norm_g[l]), w1, wkt, row(q_a_norm_g[l]),
            row(kv_a_norm_g[l]), w2, wkn, wvmt)
        osb = _sb_attention(qsb, ktsb, vsb, tri)
        omla = _mla_attention(qcat, kcat, vmt)
        last = l == depth - 1
        h = _post(h.reshape(bsz * seq, d), osb.reshape(bsz * seq, SB_WIDTH),
                  omla.reshape(bsz * seq, MLA_WIDTH), row(sb_out_norm_g[l]),
                  row(mla_out_norm_g[l]), w_o[l].astype(_BF16), row(mlp_norm_g[l]),
                  w_up[l].astype(_BF16), w_down[l].astype(_BF16), row(final_norm_g),
                  last).reshape(bsz, seq, d)
    return h
```

```python
import functools

import jax
import jax.numpy as jnp
from jax import lax
from jax.experimental import pallas as pl
from jax.experimental.pallas import tpu as pltpu

SB_HEADS = 8
SB_HEAD_DIM = 64
SB_WIDTH = SB_HEADS * SB_HEAD_DIM
SB_SCALE = SB_HEAD_DIM ** -0.5
MLA_HEADS = 4
QK_NOPE = 128
QK_ROPE = 64
V_DIM = 128
Q_LORA = 256
KV_LORA = 128
MLA_WIDTH = MLA_HEADS * V_DIM
MLA_SCALE = (QK_NOPE + QK_ROPE) ** -0.5
LOG2_E = 1.4426950408889634
ROPE_BASE = 10000.0
NORM_EPS = 1e-6

LANES = 128
MLA_QK_PAD = 256
V_EXT = V_DIM + 16
PROJ_TILE = 512
SB_TILE = 256
SB_GROUP = 2
MLA_TILE = 512
POST_TILE = 512
VMEM_LIMIT = 56 * 1024 * 1024
SB_DEAD_MASS = 158.0

_F32 = jnp.float32
_BF16 = jnp.bfloat16


def _dot(a, b):
    return jnp.dot(a, b, preferred_element_type=_F32)


def _dot_nt(a, b):
    return lax.dot_general(a, b, (((1,), (1,)), ((), ())), preferred_element_type=_F32)


def _rms(x, g):
    y = x * lax.rsqrt(jnp.mean(x * x, axis=-1, keepdims=True) + NORM_EPS)
    return y * g


def _proj_kernel(x_ref, pos_ref, invf_ref, sign_ref, g_attn_ref, w1_ref, wvt_ref, g_qa_ref,
                 g_kva_ref, w2_ref, wkn_ref, wvmt_ref,
                 qsb_ref, ksb_ref, vtsb_ref, qcat_ref, kcat_ref, vmt_ref):
    u = _rms(x_ref[0], g_attn_ref[...]).astype(_BF16)
    p1 = _dot(u, w1_ref[...])
    qsb_ref[0] = (p1[:, :SB_WIDTH] * (SB_SCALE * LOG2_E)).astype(_BF16)
    ksb_ref[0] = p1[:, SB_WIDTH:2 * SB_WIDTH].astype(_BF16)
    vtsb_ref[0] = _dot_nt(wvt_ref[...], u).astype(_BF16)

    o = 2 * SB_WIDTH
    cqn = _rms(p1[:, o:o + Q_LORA], g_qa_ref[...]).astype(_BF16)
    o += Q_LORA
    ckvn = _rms(p1[:, o:o + KV_LORA], g_kva_ref[...]).astype(_BF16)
    o += KV_LORA
    kr = p1[:, o:o + LANES]
    kr_rot = p1[:, o + LANES:o + 2 * LANES]

    ang = pos_ref[0].astype(_F32) * invf_ref[...]
    cosm = jnp.cos(ang)
    sinm = jnp.sin(ang) * sign_ref[...]

    kr_roped = (kr * cosm + kr_rot * sinm).astype(_BF16)

    p2 = _dot(cqn, w2_ref[...])
    kn = _dot(ckvn, wkn_ref[...]).astype(_BF16)
    hw = MLA_HEADS * LANES
    for h in range(MLA_HEADS):
        sl = slice(h * LANES, (h + 1) * LANES)
        lo = slice(h * MLA_QK_PAD, h * MLA_QK_PAD + LANES)
        hi = slice(h * MLA_QK_PAD + LANES, (h + 1) * MLA_QK_PAD)
        qn = p2[:, sl] * (MLA_SCALE * LOG2_E)
        qr = (p2[:, hw + h * LANES:hw + (h + 1) * LANES] * cosm
              + p2[:, 2 * hw + h * LANES:2 * hw + (h + 1) * LANES] * sinm) * (MLA_SCALE * LOG2_E)
        qcat_ref[0, :, lo] = qn.astype(_BF16)
        qcat_ref[0, :, hi] = qr.astype(_BF16)
        kcat_ref[0, :, lo] = kn[:, sl]
        kcat_ref[0, :, hi] = kr_roped
    vt = _dot_nt(wvmt_ref[...], ckvn).astype(_BF16)
    ts = vt.shape[1]
    ones_row = (lax.broadcasted_iota(jnp.int32, (V_EXT - V_DIM, ts), 0) == 0).astype(_BF16)
    for h in range(MLA_HEADS):
        vmt_ref[0, h, :V_DIM, :] = vt[h * V_DIM:(h + 1) * V_DIM]
        vmt_ref[0, h, V_DIM:, :] = ones_row


def _full(shape):
    return pl.BlockSpec(shape, lambda *_: (0,) * len(shape))


def _proj(x, pos3, invf, sign, g_attn, w1, wvt, g_qa, g_kva, w2, wkn, wvmt):
    bsz, seq, d = x.shape
    ts = PROJ_TILE
    grid = (bsz, seq // ts)
    row = lambda b, s: (b, s, 0)
    col = lambda b, s: (b, 0, s)
    out_shape = (
        jax.ShapeDtypeStruct((bsz, seq, SB_WIDTH), _BF16),
        jax.ShapeDtypeStruct((bsz, seq, SB_WIDTH), _BF16),
        jax.ShapeDtypeStruct((bsz, SB_WIDTH, seq), _BF16),
        jax.ShapeDtypeStruct((bsz, seq, MLA_HEADS * MLA_QK_PAD), _BF16),
        jax.ShapeDtypeStruct((bsz, seq, MLA_HEADS * MLA_QK_PAD), _BF16),
        jax.ShapeDtypeStruct((bsz, MLA_HEADS, V_EXT, seq), _BF16),
    )
    return pl.pallas_call(
        _proj_kernel,
        grid=grid,
        in_specs=[
            pl.BlockSpec((1, ts, d), row),
            pl.BlockSpec((1, ts, 1), row),
            _full(invf.shape), _full(sign.shape), _full(g_attn.shape), _full(w1.shape),
            _full(wvt.shape), _full(g_qa.shape), _full(g_kva.shape), _full(w2.shape),
            _full(wkn.shape), _full(wvmt.shape),
        ],
        out_specs=(
            pl.BlockSpec((1, ts, SB_WIDTH), row),
            pl.BlockSpec((1, ts, SB_WIDTH), row),
            pl.BlockSpec((1, SB_WIDTH, ts), col),
            pl.BlockSpec((1, ts, MLA_HEADS * MLA_QK_PAD), row),
            pl.BlockSpec((1, ts, MLA_HEADS * MLA_QK_PAD), row),
            pl.BlockSpec((1, MLA_HEADS, V_EXT, ts), lambda b, s: (b, 0, 0, s)),
        ),
        out_shape=out_shape,
        compiler_params=pltpu.CompilerParams(
            dimension_semantics=("arbitrary", "arbitrary"), vmem_limit_bytes=VMEM_LIMIT),
        name="proj",
    )(x, pos3, invf, sign, g_attn, w1, wvt, g_qa, g_kva, w2, wkn, wvmt)


def _sb_kernel(q_ref, k_ref, vt_ref, tri_ref, o_ref, acc_ref, mass_ref, nls_ref, spb_ref,
               ab_ref):
    t = SB_TILE
    n_q = q_ref.shape[1] // t
    tri = tri_ref[...]
    sub = lax.broadcasted_iota(jnp.int32, (LANES, t), 0)
    first = sub < SB_HEAD_DIM

    def strict_mask():
        key = lax.broadcasted_iota(jnp.int32, (t, t), 0)
        qry = lax.broadcasted_iota(jnp.int32, (t, t), 1)
        return key < qry

    def stage_softplus(c, qt, j, masked):
        start = pl.multiple_of(j * t, t)
        z = _dot(k_ref[0, pl.ds(start, t), :], qt)
        pos = jnp.maximum(z, 0.0)
        neg = z - pos
        lg = jnp.log2(1.0 + jnp.exp2(neg - pos))
        sp = pos + lg
        if masked:
            sp = jnp.where(strict_mask(), sp, 0.0)
        spb_ref[c] = sp.astype(_BF16)
        nls_ref[c] = neg - lg
        return sp[:1, :]

    def stage_weights(c, masked, sp_row0):
        excl = _dot(tri, spb_ref[c])
        a = jnp.exp2(nls_ref[c] - excl)
        if masked:
            a = jnp.where(strict_mask(), a, 0.0)
        ab_ref[c] = a.astype(_BF16)
        return excl[:1, :] + sp_row0

    def stage_pv(c, j):
        start = pl.multiple_of(j * t, t)
        return _dot(vt_ref[0, :, pl.ds(start, t)], ab_ref[c])

    def load_qt(i):
        start = pl.multiple_of(i * t, t)
        qt = q_ref[0, pl.ds(start, t), :].astype(_F32).T
        zero = jnp.zeros_like(qt)
        return (jnp.where(first, qt, zero).astype(_BF16), jnp.where(first, zero, qt).astype(_BF16))

    def group(g, carry):
        chains = []
        for u in range(SB_GROUP):
            i = g * SB_GROUP + u
            qts = load_qt(i)
            for hh in range(2):
                chains.append((qts[hh], i, True))
                chains.append((qts[hh], jnp.maximum(i - 1, 0), False))
        rows0 = [stage_softplus(c, *chain) for c, chain in enumerate(chains)]
        tots = [stage_weights(c, masked, rows0[c]) for c, (_, _, masked) in enumerate(chains)]
        pvs = [stage_pv(c, j) for c, (_, j, _) in enumerate(chains)]
        for u in range(SB_GROUP):
            has_prev = (g * SB_GROUP + u > 0).astype(_F32)
            for hh in range(2):
                d = 4 * u + 2 * hh
                acc_ref[u, hh] = pvs[d] + (jnp.exp2(-tots[d]) * has_prev) * pvs[d + 1]
                mass_ref[u, hh] = tots[d] + has_prev * tots[d + 1]
        for u in range(SB_GROUP):
            i = g * SB_GROUP + u

            def min_mass():
                return jnp.min(jnp.minimum(mass_ref[u, 0], mass_ref[u, 1]))

            def cond(c):
                j, mm = c
                return jnp.logical_and(j >= 0, mm < SB_DEAD_MASS)

            def body(c):
                j, _ = c
                qts = load_qt(i)
                for hh in range(2):
                    tot = stage_weights(hh, False, stage_softplus(hh, qts[hh], j, False))
                    mass = mass_ref[u, hh]
                    acc_ref[u, hh] += jnp.exp2(-mass) * stage_pv(hh, j)
                    mass_ref[u, hh] = mass + tot
                return j - 1, min_mass()

            lax.while_loop(cond, body, (i - 2, min_mass()))
            out_t = jnp.where(first, acc_ref[u, 0], acc_ref[u, 1])
            o_ref[0, pl.ds(pl.multiple_of(i * t, t), t), :] = out_t.T
        return carry

    lax.fori_loop(0, n_q // SB_GROUP, group, 0)


def _sb_attention(qsb, ksb, vtsb, tri):
    bsz, seq, _ = qsb.shape
    t = SB_TILE
    grid = (bsz, SB_WIDTH // LANES)
    return pl.pallas_call(
        _sb_kernel,
        grid=grid,
        in_specs=[
            pl.BlockSpec((1, seq, LANES), lambda b, p: (b, 0, p)),
            pl.BlockSpec((1, seq, LANES), lambda b, p: (b, 0, p)),
            pl.BlockSpec((1, LANES, seq), lambda b, p: (b, p, 0)),
            pl.BlockSpec((t, t), lambda b, p: (0, 0)),
        ],
        out_specs=pl.BlockSpec((1, seq, LANES), lambda b, p: (b, 0, p)),
        out_shape=jax.ShapeDtypeStruct((bsz, seq, SB_WIDTH), _F32),
        scratch_shapes=[pltpu.VMEM((SB_GROUP, 2, LANES, t), _F32),
                        pltpu.VMEM((SB_GROUP, 2, 1, t), _F32),
                        pltpu.VMEM((4 * SB_GROUP, t, t), _F32),
                        pltpu.VMEM((4 * SB_GROUP, t, t), _BF16),
                        pltpu.VMEM((4 * SB_GROUP, t, t), _BF16)],
        compiler_params=pltpu.CompilerParams(
            dimension_semantics=("arbitrary", "arbitrary"), vmem_limit_bytes=VMEM_LIMIT),
        name="sb_attn",
    )(qsb, ksb, vtsb, tri)


def _mla_kernel(q_ref, k_ref, vt_ref, o_ref, acc_ref, m_ref, sa_ref, sb_ref):
    t = MLA_TILE
    n_q = q_ref.shape[1] // t

    def q_tile(i, carry):
        qstart = pl.multiple_of(i * t, t)
        qt = q_ref[0, pl.ds(qstart, t), :].astype(_F32).T.astype(_BF16)
        acc_ref[...] = jnp.zeros_like(acc_ref)
        m_ref[...] = jnp.full_like(m_ref, jnp.finfo(_F32).min)

        def scores_into(dst_ref, j):
            start = pl.multiple_of(j * t, t)
            dst_ref[...] = _dot(k_ref[0, pl.ds(start, t), :], qt)

        def update(src_ref, j, masked):
            start = pl.multiple_of(j * t, t)
            vt = vt_ref[0, 0, :, pl.ds(start, t)]
            s = src_ref[...]
            if masked:
                key = lax.broadcasted_iota(jnp.int32, (t, t), 0)
                qry = lax.broadcasted_iota(jnp.int32, (t, t), 1)
                s = jnp.where(key <= qry, s, jnp.finfo(_F32).min)
            m_old = m_ref[...]
            m_new = jnp.maximum(m_old, jnp.max(s, axis=0, keepdims=True))
            p = jnp.exp2(s - m_new)
            alpha = jnp.exp2(m_old - m_new)
            acc_ref[...] = alpha * acc_ref[...] + _dot(vt, p.astype(_BF16))
            m_ref[...] = m_new

        scores_into(sa_ref, 0)

        def pair(jj, c):
            j = 2 * jj
            scores_into(sb_ref, j + 1)
            update(sa_ref, j, False)
            scores_into(sa_ref, j + 2)
            update(sb_ref, j + 1, False)
            return c

        lax.fori_loop(0, i // 2, pair, 0)

        @pl.when(i % 2 == 1)
        def _():
            scores_into(sb_ref, i)
            update(sa_ref, i - 1, False)
            update(sb_ref, i, True)

        @pl.when(i % 2 == 0)
        def _():
            update(sa_ref, i, True)

        o_ref[0, pl.ds(qstart, t), :] = (acc_ref[:V_DIM, :] / acc_ref[V_DIM:V_DIM + 1, :]).T
        return carry

    lax.fori_loop(0, n_q, q_tile, 0)


def _mla_attention(qcat, kcat, vmt):
    bsz, seq, _ = qcat.shape
    t = MLA_TILE
    grid = (bsz, MLA_HEADS)
    return pl.pallas_call(
        _mla_kernel,
        grid=grid,
        in_specs=[
            pl.BlockSpec((1, seq, MLA_QK_PAD), lambda b, h: (b, 0, h)),
            pl.BlockSpec((1, seq, MLA_QK_PAD), lambda b, h: (b, 0, h)),
            pl.BlockSpec((1, 1, V_EXT, seq), lambda b, h: (b, h, 0, 0)),
        ],
        out_specs=pl.BlockSpec((1, seq, V_DIM), lambda b, h: (b, 0, h)),
        out_shape=jax.ShapeDtypeStruct((bsz, seq, MLA_WIDTH), _F32),
        scratch_shapes=[pltpu.VMEM((V_EXT, t), _F32), pltpu.VMEM((1, t), _F32),
                        pltpu.VMEM((t, t), _F32), pltpu.VMEM((t, t), _F32)],
        compiler_params=pltpu.CompilerParams(
            dimension_semantics=("arbitrary", "arbitrary"), vmem_limit_bytes=VMEM_LIMIT),
        name="mla_attn",
    )(qcat, kcat, vmt)


def _post_kernel(x_ref, osb_ref, omla_ref, g_sb_ref, g_mla_ref, wo_ref, g_mlp_ref, wup_ref,
                 wdown_ref, g_out_ref, out_ref, *, apply_out_norm):
    ms = _rms(osb_ref[...], g_sb_ref[...]).astype(_BF16)
    mm = _rms(omla_ref[...], g_mla_ref[...]).astype(_BF16)
    h1 = x_ref[...] + _dot(jnp.concatenate([ms, mm], axis=1), wo_ref[...])
    v = _rms(h1, g_mlp_ref[...]).astype(_BF16)
    hid = jnp.square(jnp.maximum(_dot(v, wup_ref[...]), 0.0)).astype(_BF16)
    h2 = h1 + _dot(hid, wdown_ref[...])
    out_ref[...] = _rms(h2, g_out_ref[...]) if apply_out_norm else h2


def _post(x2, osb2, omla2, g_sb, g_mla, wo, g_mlp, wup, wdown, g_out, apply_out_norm):
    n, d = x2.shape
    tm = POST_TILE
    once = pl.Buffered(1)
    rows = lambda w: pl.BlockSpec((tm, w), lambda r: (r, 0))
    const = lambda a: pl.BlockSpec(a.shape, lambda r: (0, 0), pipeline_mode=once)
    return pl.pallas_call(
        functools.partial(_post_kernel, apply_out_norm=apply_out_norm),
        grid=(n // tm,),
        in_specs=[rows(d), rows(SB_WIDTH), rows(MLA_WIDTH), const(g_sb), const(g_mla), const(wo),
                  const(g_mlp), const(wup), const(wdown), const(g_out)],
        out_specs=rows(d),
        out_shape=jax.ShapeDtypeStruct((n, d), _F32),
        compiler_params=pltpu.CompilerParams(
            dimension_semantics=("arbitrary",), vmem_limit_bytes=VMEM_LIMIT),
        name="post",
    )(x2, osb2, omla2, g_sb, g_mla, wo, g_mlp, wup, wdown, g_out)


def _swap_halves(w):
    half = QK_ROPE // 2
    return jnp.concatenate([w[..., half:], w[..., :half]], axis=-1)


def _layer_weights(w_in, w_q_b, w_kv_b):
    d = w_in.shape[0]
    s2, s3 = 2 * SB_WIDTH, 3 * SB_WIDTH
    s5 = s3 + Q_LORA + KV_LORA
    wkr = w_in[:, s5:]
    pad = jnp.zeros((d, LANES - QK_ROPE), w_in.dtype)
    w1 = jnp.concatenate([w_in[:, :s2], w_in[:, s3:s5], wkr, pad, _swap_halves(wkr), pad],
                         axis=1).astype(_BF16)
    wvt = w_in[:, s2:s3].T.astype(_BF16)
    wqb = w_q_b.reshape(Q_LORA, MLA_HEADS, QK_NOPE + QK_ROPE)
    qn = wqb[:, :, :QK_NOPE].reshape(Q_LORA, MLA_HEADS * QK_NOPE)
    qr = wqb[:, :, QK_NOPE:]
    qpad = jnp.zeros((Q_LORA, MLA_HEADS, LANES - QK_ROPE), w_q_b.dtype)
    qr_pad = jnp.concatenate([qr, qpad], axis=-1).reshape(Q_LORA, MLA_HEADS * LANES)
    qrr_pad = jnp.concatenate([_swap_halves(qr), qpad], axis=-1).reshape(Q_LORA, MLA_HEADS * LANES)
    w2 = jnp.concatenate([qn, qr_pad, qrr_pad], axis=1).astype(_BF16)
    wkvb = w_kv_b.reshape(KV_LORA, MLA_HEADS, QK_NOPE + V_DIM)
    wkn = wkvb[:, :, :QK_NOPE].reshape(KV_LORA, MLA_HEADS * QK_NOPE).astype(_BF16)
    wvmt = wkvb[:, :, QK_NOPE:].reshape(KV_LORA, MLA_HEADS * V_DIM).T.astype(_BF16)
    return w1, wvt, w2, wkn, wvmt


def kernel(x, positions, attn_norm_g, w_in, q_a_norm_g, w_q_b, kv_a_norm_g, w_kv_b, sb_out_norm_g,
           mla_out_norm_g, w_o, mlp_norm_g, w_up, w_down, final_norm_g):
    bsz, seq, d = x.shape
    depth = w_in.shape[0]
    assert seq % PROJ_TILE == 0 and seq % (SB_TILE * SB_GROUP) == 0 and seq % MLA_TILE == 0
    assert (bsz * seq) % POST_TILE == 0

    half = QK_ROPE // 2
    inv_freq = ROPE_BASE ** (-jnp.arange(half, dtype=_F32) / half)
    invf = jnp.tile(inv_freq, LANES // half)[None, :]
    sign = jnp.tile(jnp.concatenate([-jnp.ones((half,), _F32), jnp.ones((half,), _F32)]),
                    LANES // QK_ROPE)[None, :]
    pos3 = positions[:, :, None]
    r = lax.broadcasted_iota(jnp.int32, (SB_TILE, SB_TILE), 0)
    c = lax.broadcasted_iota(jnp.int32, (SB_TILE, SB_TILE), 1)
    tri = (c > r).astype(_BF16)

    row = lambda g: g[None, :]
    h = x
    for l in range(depth):
        w1, wvt, w2, wkn, wvmt = _layer_weights(w_in[l], w_q_b[l], w_kv_b[l])
        qsb, ksb, vtsb, qcat, kcat, vmt = _proj(
            h, pos3, invf, sign, row(attn_norm_g[l]), w1, wvt, row(q_a_norm_g[l]),
            row(kv_a_norm_g[l]), w2, wkn, wvmt)
        osb = _sb_attention(qsb, ksb, vtsb, tri)
        omla = _mla_attention(qcat, kcat, vmt)
        last = l == depth - 1
        h = _post(h.reshape(bsz * seq, d), osb.reshape(bsz * seq, SB_WIDTH),
                  omla.reshape(bsz * seq, MLA_WIDTH), row(sb_out_norm_g[l]),
                  row(mla_out_norm_g[l]), w_o[l].astype(_BF16), row(mlp_norm_g[l]),
                  w_up[l].astype(_BF16), w_down[l].astype(_BF16), row(final_norm_g),
                  last).reshape(bsz, seq, d)
    return h
```

```python
import functools

import jax
import jax.numpy as jnp
from jax import lax
from jax.experimental import pallas as pl
from jax.experimental.pallas import tpu as pltpu

SB_HEADS = 8
SB_HEAD_DIM = 64
SB_WIDTH = SB_HEADS * SB_HEAD_DIM
SB_SCALE = SB_HEAD_DIM ** -0.5
MLA_HEADS = 4
QK_NOPE = 128
QK_ROPE = 64
V_DIM = 128
Q_LORA = 256
KV_LORA = 128
MLA_WIDTH = MLA_HEADS * V_DIM
MLA_SCALE = (QK_NOPE + QK_ROPE) ** -0.5
LOG2_E = 1.4426950408889634
ROPE_BASE = 10000.0
NORM_EPS = 1e-6

LANES = 128
MLA_QK_PAD = 256
V_EXT = V_DIM + 16
PROJ_TILE = 512
SB_TILE = 256
SB_GROUP = 2
MLA_TILE = 512
POST_TILE = 512
VMEM_LIMIT = 56 * 1024 * 1024
SB_DEAD_MASS = 158.0
SB_MASKED_SCORE = -1e30
MLA_MASK_BIAS = -1e30

_F32 = jnp.float32
_BF16 = jnp.bfloat16


def _dot(a, b):
    return jnp.dot(a, b, preferred_element_type=_F32)


def _dot_nt(a, b):
    return lax.dot_general(a, b, (((1,), (1,)), ((), ())), preferred_element_type=_F32)


def _rms(x, g):
    y = x * lax.rsqrt(jnp.mean(x * x, axis=-1, keepdims=True) + NORM_EPS)
    return y * g


def _proj_kernel(x_ref, pos_ref, invf_ref, sign_ref, g_attn_ref, w1_ref, wvt_ref, wkrt_ref,
                 g_qa_ref, g_kva_ref, w2t_ref, wkn_ref, wvmt_ref,
                 qsb_ref, ksb_ref, vtsb_ref, qtcat_ref, kcat_ref, vmt_ref):
    u = _rms(x_ref[0], g_attn_ref[...]).astype(_BF16)
    p1 = _dot(u, w1_ref[...])
    qsb_ref[0] = (p1[:, :SB_WIDTH] * (SB_SCALE * LOG2_E)).astype(_BF16)
    ksb_ref[0] = p1[:, SB_WIDTH:2 * SB_WIDTH].astype(_BF16)
    vtsb_ref[0] = _dot_nt(wvt_ref[...], u).astype(_BF16)

    o = 2 * SB_WIDTH
    cqn = _rms(p1[:, o:o + Q_LORA], g_qa_ref[...]).astype(_BF16)
    o += Q_LORA
    ckvn = _rms(p1[:, o:o + KV_LORA], g_kva_ref[...]).astype(_BF16)

    ang = invf_ref[...] * pos_ref[0].astype(_F32)
    cost = jnp.cos(ang)
    sint = jnp.sin(ang) * sign_ref[...]

    krt = _dot_nt(wkrt_ref[...], u)
    kr_roped = (krt[:LANES] * cost + krt[LANES:] * sint).T.astype(_BF16)

    p2t = _dot_nt(w2t_ref[...], cqn)
    kn = _dot(ckvn, wkn_ref[...]).astype(_BF16)
    hw = MLA_HEADS * LANES
    for h in range(MLA_HEADS):
        sl = slice(h * LANES, (h + 1) * LANES)
        qn = p2t[sl] * (MLA_SCALE * LOG2_E)
        qr = (p2t[hw + h * LANES:hw + (h + 1) * LANES] * cost
              + p2t[2 * hw + h * LANES:2 * hw + (h + 1) * LANES] * sint) * (MLA_SCALE * LOG2_E)
        qtcat_ref[0, h, :LANES, :] = qn.astype(_BF16)
        qtcat_ref[0, h, LANES:, :] = qr.astype(_BF16)
        kcat_ref[0, :, h * MLA_QK_PAD:h * MLA_QK_PAD + LANES] = kn[:, sl]
        kcat_ref[0, :, h * MLA_QK_PAD + LANES:(h + 1) * MLA_QK_PAD] = kr_roped
    vt = _dot_nt(wvmt_ref[...], ckvn).astype(_BF16)
    ts = vt.shape[1]
    ones_row = (lax.broadcasted_iota(jnp.int32, (V_EXT - V_DIM, ts), 0) == 0).astype(_BF16)
    for h in range(MLA_HEADS):
        vmt_ref[0, h, :V_DIM, :] = vt[h * V_DIM:(h + 1) * V_DIM]
        vmt_ref[0, h, V_DIM:, :] = ones_row


def _full(shape):
    return pl.BlockSpec(shape, lambda *_: (0,) * len(shape))


def _proj(x, pos_row, invf, sign, g_attn, w1, wvt, wkrt, g_qa, g_kva, w2t, wkn, wvmt):
    bsz, seq, d = x.shape
    ts = PROJ_TILE
    grid = (bsz, seq // ts)
    row = lambda b, s: (b, s, 0)
    col = lambda b, s: (b, 0, s)
    out_shape = (
        jax.ShapeDtypeStruct((bsz, seq, SB_WIDTH), _BF16),
        jax.ShapeDtypeStruct((bsz, seq, SB_WIDTH), _BF16),
        jax.ShapeDtypeStruct((bsz, SB_WIDTH, seq), _BF16),
        jax.ShapeDtypeStruct((bsz, MLA_HEADS, MLA_QK_PAD, seq), _BF16),
        jax.ShapeDtypeStruct((bsz, seq, MLA_HEADS * MLA_QK_PAD), _BF16),
        jax.ShapeDtypeStruct((bsz, MLA_HEADS, V_EXT, seq), _BF16),
    )
    heads_col = lambda b, s: (b, 0, 0, s)
    return pl.pallas_call(
        _proj_kernel,
        grid=grid,
        in_specs=[
            pl.BlockSpec((1, ts, d), row),
            pl.BlockSpec((1, 1, ts), col),
            _full(invf.shape), _full(sign.shape), _full(g_attn.shape), _full(w1.shape),
            _full(wvt.shape), _full(wkrt.shape), _full(g_qa.shape), _full(g_kva.shape),
            _full(w2t.shape), _full(wkn.shape), _full(wvmt.shape),
        ],
        out_specs=(
            pl.BlockSpec((1, ts, SB_WIDTH), row),
            pl.BlockSpec((1, ts, SB_WIDTH), row),
            pl.BlockSpec((1, SB_WIDTH, ts), col),
            pl.BlockSpec((1, MLA_HEADS, MLA_QK_PAD, ts), heads_col),
            pl.BlockSpec((1, ts, MLA_HEADS * MLA_QK_PAD), row),
            pl.BlockSpec((1, MLA_HEADS, V_EXT, ts), heads_col),
        ),
        out_shape=out_shape,
        compiler_params=pltpu.CompilerParams(
            dimension_semantics=("arbitrary", "arbitrary"), vmem_limit_bytes=VMEM_LIMIT),
        name="proj",
    )(x, pos_row, invf, sign, g_attn, w1, wvt, wkrt, g_qa, g_kva, w2t, wkn, wvmt)


def _sb_kernel(q_ref, k_ref, vt_ref, tri_ref, o_ref, acc_ref, mass_ref, nls_ref, spb_ref,
               ab_ref):
    t = SB_TILE
    n_q = q_ref.shape[1] // t
    tri = tri_ref[...]
    sub = lax.broadcasted_iota(jnp.int32, (LANES, t), 0)
    first = sub < SB_HEAD_DIM

    def strict_mask():
        key = lax.broadcasted_iota(jnp.int32, (t, t), 0)
        qry = lax.broadcasted_iota(jnp.int32, (t, t), 1)
        return key < qry

    def stage_softplus(c, qt, j, masked):
        start = pl.multiple_of(j * t, t)
        z = _dot(k_ref[0, pl.ds(start, t), :], qt)
        if masked:
            z = jnp.where(strict_mask(), z, SB_MASKED_SCORE)
        zb = z.astype(_BF16)
        pos = jnp.maximum(zb, 0.0)
        neg = zb - pos
        lg = jnp.log2((1.0 + jnp.exp2(neg - pos)).astype(_F32)).astype(_BF16)
        sp = pos + lg
        spb_ref[c] = sp
        nls_ref[c] = neg - lg
        return sp[:1, :].astype(_F32)

    def stage_weights(c, sp_row0):
        excl = _dot(tri, spb_ref[c])
        ab_ref[c] = jnp.exp2(nls_ref[c] - excl.astype(_BF16))
        return excl[:1, :] + sp_row0

    def stage_pv(c, j):
        start = pl.multiple_of(j * t, t)
        return _dot(vt_ref[0, :, pl.ds(start, t)], ab_ref[c])

    def load_qt(i):
        start = pl.multiple_of(i * t, t)
        qt = q_ref[0, pl.ds(start, t), :].astype(_F32).T
        zero = jnp.zeros_like(qt)
        return (jnp.where(first, qt, zero).astype(_BF16), jnp.where(first, zero, qt).astype(_BF16))

    def group(g, carry):
        chains = []
        for u in range(SB_GROUP):
            i = g * SB_GROUP + u
            qts = load_qt(i)
            for hh in range(2):
                chains.append((qts[hh], i, True))
                chains.append((qts[hh], jnp.maximum(i - 1, 0), False))
        rows0 = [stage_softplus(c, *chain) for c, chain in enumerate(chains)]
        tots = [stage_weights(c, rows0[c]) for c in range(len(chains))]
        pvs = [stage_pv(c, j) for c, (_, j, _) in enumerate(chains)]
        for u in range(SB_GROUP):
            has_prev = jnp.where(g * SB_GROUP + u > 0, 1.0, 0.0).astype(_F32)
            for hh in range(2):
                d = 4 * u + 2 * hh
                acc_ref[u, hh] = pvs[d] + (jnp.exp2(-tots[d]) * has_prev) * pvs[d + 1]
                mass_ref[u, hh] = tots[d] + has_prev * tots[d + 1]
        for u in range(SB_GROUP):
            i = g * SB_GROUP + u

            def min_mass():
                return jnp.min(jnp.minimum(mass_ref[u, 0], mass_ref[u, 1]))

            def cond(c):
                j, mm = c
                return jnp.logical_and(j >= 0, mm < SB_DEAD_MASS)

            def body(c):
                j, _ = c
                qts = load_qt(i)
                for hh in range(2):
                    tot = stage_weights(hh, stage_softplus(hh, qts[hh], j, False))
                    mass = mass_ref[u, hh]
                    acc_ref[u, hh] += jnp.exp2(-mass) * stage_pv(hh, j)
                    mass_ref[u, hh] = mass + tot
                return j - 1, min_mass()

            lax.while_loop(cond, body, (i - 2, min_mass()))
            out_t = jnp.where(first, acc_ref[u, 0], acc_ref[u, 1])
            o_ref[0, pl.ds(pl.multiple_of(i * t, t), t), :] = out_t.T
        return carry

    lax.fori_loop(0, n_q // SB_GROUP, group, 0)


def _sb_attention(qsb, ksb, vtsb, tri):
    bsz, seq, _ = qsb.shape
    t = SB_TILE
    grid = (bsz, SB_WIDTH // LANES)
    return pl.pallas_call(
        _sb_kernel,
        grid=grid,
        in_specs=[
            pl.BlockSpec((1, seq, LANES), lambda b, p: (b, 0, p)),
            pl.BlockSpec((1, seq, LANES), lambda b, p: (b, 0, p)),
            pl.BlockSpec((1, LANES, seq), lambda b, p: (b, p, 0)),
            pl.BlockSpec((t, t), lambda b, p: (0, 0)),
        ],
        out_specs=pl.BlockSpec((1, seq, LANES), lambda b, p: (b, 0, p)),
        out_shape=jax.ShapeDtypeStruct((bsz, seq, SB_WIDTH), _F32),
        scratch_shapes=[pltpu.VMEM((SB_GROUP, 2, LANES, t), _F32),
                        pltpu.VMEM((SB_GROUP, 2, 1, t), _F32),
                        pltpu.VMEM((4 * SB_GROUP, t, t), _BF16),
                        pltpu.VMEM((4 * SB_GROUP, t, t), _BF16),
                        pltpu.VMEM((4 * SB_GROUP, t, t), _BF16)],
        compiler_params=pltpu.CompilerParams(
            dimension_semantics=("arbitrary", "arbitrary"), vmem_limit_bytes=VMEM_LIMIT),
        name="sb_attn",
    )(qsb, ksb, vtsb, tri)


def _mla_kernel(qt_ref, k_ref, vt_ref, bias_ref, o_ref, acc_ref, m_ref, sa_ref, sb_ref):
    t = MLA_TILE
    n_q = k_ref.shape[1] // t
    n_off = n_q * (n_q - 1) // 2

    def scores_into(dst_ref, i, j):
        i = jnp.minimum(i, n_q - 1)
        qs = pl.multiple_of(i * t, t)
        ks = pl.multiple_of(j * t, t)
        dst_ref[...] = _dot(k_ref[0, pl.ds(ks, t), :], qt_ref[0, 0, :, pl.ds(qs, t)])

    def update(src_ref, i, j, diagonal):
        ks = pl.multiple_of(j * t, t)
        vt = vt_ref[0, 0, :, pl.ds(ks, t)]
        s = src_ref[...]
        if diagonal:
            s = s + bias_ref[...]
        m_old = m_ref[i]
        m_new = jnp.maximum(m_old, jnp.max(s, axis=0, keepdims=True))
        p = jnp.exp2(s - m_new)
        alpha = jnp.exp2(m_old - m_new)
        acc = alpha * acc_ref[i] + _dot(vt, p.astype(_BF16))
        if diagonal:
            qs = pl.multiple_of(i * t, t)
            o_ref[0, pl.ds(qs, t), :] = (acc[:V_DIM, :] / acc[V_DIM:V_DIM + 1, :]).T
        else:
            acc_ref[i] = acc
            m_ref[i] = m_new

    acc_ref[...] = jnp.zeros_like(acc_ref)
    m_ref[...] = jnp.full_like(m_ref, jnp.finfo(_F32).min)

    def advance(i, j):
        wrap = j + 1 >= i
        return jnp.where(wrap, i + 1, i), jnp.where(wrap, 0, j + 1)

    def off_pair(_, carry):
        i, j = carry
        i1, j1 = advance(i, j)
        scores_into(sb_ref, i1, j1)
        update(sa_ref, i, j, False)
        i2, j2 = advance(i1, j1)
        scores_into(sa_ref, i2, j2)
        update(sb_ref, i1, j1, False)
        return i2, j2

    if n_off:
        scores_into(sa_ref, 1, 0)
        i, j = lax.fori_loop(0, n_off // 2, off_pair, (jnp.int32(1), jnp.int32(0)))
        if n_off % 2:
            update(sa_ref, i, j, False)

    def diag_pair(ii, carry):
        i = 2 * ii
        scores_into(sb_ref, i + 1, i + 1)
        update(sa_ref, i, i, True)
        scores_into(sa_ref, i + 2, jnp.minimum(i + 2, n_q - 1))
        update(sb_ref, i + 1, i + 1, True)
        return carry

    scores_into(sa_ref, 0, 0)
    lax.fori_loop(0, n_q // 2, diag_pair, 0)
    if n_q % 2:
        update(sa_ref, n_q - 1, n_q - 1, True)


def _mla_attention(qtcat, kcat, vmt, bias):
    bsz, seq, _ = kcat.shape
    t = MLA_TILE
    grid = (bsz, MLA_HEADS)
    return pl.pallas_call(
        _mla_kernel,
        grid=grid,
        in_specs=[
            pl.BlockSpec((1, 1, MLA_QK_PAD, seq), lambda b, h: (b, h, 0, 0)),
            pl.BlockSpec((1, seq, MLA_QK_PAD), lambda b, h: (b, 0, h)),
            pl.BlockSpec((1, 1, V_EXT, seq), lambda b, h: (b, h, 0, 0)),
            pl.BlockSpec((t, t), lambda b, h: (0, 0)),
        ],
        out_specs=pl.BlockSpec((1, seq, V_DIM), lambda b, h: (b, 0, h)),
        out_shape=jax.ShapeDtypeStruct((bsz, seq, MLA_WIDTH), _F32),
        scratch_shapes=[pltpu.VMEM((seq // t, V_EXT, t), _F32), pltpu.VMEM((seq // t, 1, t), _F32),
                        pltpu.VMEM((t, t), _F32), pltpu.VMEM((t, t), _F32)],
        compiler_params=pltpu.CompilerParams(
            dimension_semantics=("arbitrary", "arbitrary"), vmem_limit_bytes=VMEM_LIMIT),
        name="mla_attn",
    )(qtcat, kcat, vmt, bias)


def _post_kernel(x_ref, osb_ref, omla_ref, g_sb_ref, g_mla_ref, wo_ref, g_mlp_ref, wup_ref,
                 wdown_ref, g_out_ref, out_ref, *, apply_out_norm):
    ms = _rms(osb_ref[...], g_sb_ref[...]).astype(_BF16)
    mm = _rms(omla_ref[...], g_mla_ref[...]).astype(_BF16)
    h1 = x_ref[...] + _dot(jnp.concatenate([ms, mm], axis=1), wo_ref[...])
    v = _rms(h1, g_mlp_ref[...]).astype(_BF16)
    hid = jnp.square(jnp.maximum(_dot(v, wup_ref[...]), 0.0)).astype(_BF16)
    h2 = h1 + _dot(hid, wdown_ref[...])
    out_ref[...] = _rms(h2, g_out_ref[...]) if apply_out_norm else h2


def _post(x2, osb2, omla2, g_sb, g_mla, wo, g_mlp, wup, wdown, g_out, apply_out_norm):
    n, d = x2.shape
    tm = POST_TILE
    once = pl.Buffered(1)
    rows = lambda w: pl.BlockSpec((tm, w), lambda r: (r, 0))
    const = lambda a: pl.BlockSpec(a.shape, lambda r: (0, 0), pipeline_mode=once)
    return pl.pallas_call(
        functools.partial(_post_kernel, apply_out_norm=apply_out_norm),
        grid=(n // tm,),
        in_specs=[rows(d), rows(SB_WIDTH), rows(MLA_WIDTH), const(g_sb), const(g_mla), const(wo),
                  const(g_mlp), const(wup), const(wdown), const(g_out)],
        out_specs=rows(d),
        out_shape=jax.ShapeDtypeStruct((n, d), _F32),
        compiler_params=pltpu.CompilerParams(
            dimension_semantics=("arbitrary",), vmem_limit_bytes=VMEM_LIMIT),
        name="post",
    )(x2, osb2, omla2, g_sb, g_mla, wo, g_mlp, wup, wdown, g_out)


def _swap_halves(w):
    half = QK_ROPE // 2
    return jnp.concatenate([w[..., half:], w[..., :half]], axis=-1)


def _layer_weights(w_in, w_q_b, w_kv_b):
    d = w_in.shape[0]
    s2, s3 = 2 * SB_WIDTH, 3 * SB_WIDTH
    s5 = s3 + Q_LORA + KV_LORA
    w1 = jnp.concatenate([w_in[:, :s2], w_in[:, s3:s5]], axis=1).astype(_BF16)
    wvt = w_in[:, s2:s3].T.astype(_BF16)
    wkr = w_in[:, s5:]
    pad = jnp.zeros((d, LANES - QK_ROPE), w_in.dtype)
    wkrt = jnp.concatenate([wkr, pad, _swap_halves(wkr), pad], axis=1).T.astype(_BF16)
    wqb = w_q_b.reshape(Q_LORA, MLA_HEADS, QK_NOPE + QK_ROPE)
    qn = wqb[:, :, :QK_NOPE].reshape(Q_LORA, MLA_HEADS * QK_NOPE)
    qr = wqb[:, :, QK_NOPE:]
    qpad = jnp.zeros((Q_LORA, MLA_HEADS, LANES - QK_ROPE), w_q_b.dtype)
    qr_pad = jnp.concatenate([qr, qpad], axis=-1).reshape(Q_LORA, MLA_HEADS * LANES)
    qrr_pad = jnp.concatenate([_swap_halves(qr), qpad], axis=-1).reshape(Q_LORA, MLA_HEADS * LANES)
    w2t = jnp.concatenate([qn, qr_pad, qrr_pad], axis=1).T.astype(_BF16)
    wkvb = w_kv_b.reshape(KV_LORA, MLA_HEADS, QK_NOPE + V_DIM)
    wkn = wkvb[:, :, :QK_NOPE].reshape(KV_LORA, MLA_HEADS * QK_NOPE).astype(_BF16)
    wvmt = wkvb[:, :, QK_NOPE:].reshape(KV_LORA, MLA_HEADS * V_DIM).T.astype(_BF16)
    return w1, wvt, wkrt, w2t, wkn, wvmt


def kernel(x, positions, attn_norm_g, w_in, q_a_norm_g, w_q_b, kv_a_norm_g, w_kv_b, sb_out_norm_g,
           mla_out_norm_g, w_o, mlp_norm_g, w_up, w_down, final_norm_g):
    bsz, seq, d = x.shape
    depth = w_in.shape[0]
    assert seq % PROJ_TILE == 0 and seq % (SB_TILE * SB_GROUP) == 0 and seq % MLA_TILE == 0
    assert (bsz * seq) % POST_TILE == 0

    half = QK_ROPE // 2
    inv_freq = ROPE_BASE ** (-jnp.arange(half, dtype=_F32) / half)
    invf = jnp.tile(inv_freq, LANES // half)[:, None]
    sign = jnp.tile(jnp.concatenate([-jnp.ones((half,), _F32), jnp.ones((half,), _F32)]),
                    LANES // QK_ROPE)[:, None]
    pos_row = positions[:, None, :]
    r = lax.broadcasted_iota(jnp.int32, (SB_TILE, SB_TILE), 0)
    c = lax.broadcasted_iota(jnp.int32, (SB_TILE, SB_TILE), 1)
    tri = (c > r).astype(_BF16)
    key = lax.broadcasted_iota(jnp.int32, (MLA_TILE, MLA_TILE), 0)
    qry = lax.broadcasted_iota(jnp.int32, (MLA_TILE, MLA_TILE), 1)
    bias = jnp.where(key <= qry, 0.0, MLA_MASK_BIAS).astype(_F32)

    row = lambda g: g[None, :]
    h = x
    for l in range(depth):
        w1, wvt, wkrt, w2t, wkn, wvmt = _layer_weights(w_in[l], w_q_b[l], w_kv_b[l])
        qsb, ksb, vtsb, qtcat, kcat, vmt = _proj(
            h, pos_row, invf, sign, row(attn_norm_g[l]), w1, wvt, wkrt, row(q_a_norm_g[l]),
            row(kv_a_norm_g[l]), w2t, wkn, wvmt)
        osb = _sb_attention(qsb, ksb, vtsb, tri)
        omla = _mla_attention(qtcat, kcat, vmt, bias)
        last = l == depth - 1
        h = _post(h.reshape(bsz * seq, d), osb.reshape(bsz * seq, SB_WIDTH),
                  omla.reshape(bsz * seq, MLA_WIDTH), row(sb_out_norm_g[l]),
                  row(mla_out_norm_g[l]), w_o[l].astype(_BF16), row(mlp_norm_g[l]),
                  w_up[l].astype(_BF16), w_down[l].astype(_BF16), row(final_norm_g),
                  last).reshape(bsz, seq, d)
    return h
```

```python
import functools

import jax
import jax.numpy as jnp
from jax import lax
from jax.experimental import pallas as pl
from jax.experimental.pallas import tpu as pltpu

SB_HEADS = 8
SB_HEAD_DIM = 64
SB_WIDTH = SB_HEADS * SB_HEAD_DIM
SB_SCALE = SB_HEAD_DIM ** -0.5
MLA_HEADS = 4
QK_NOPE = 128
QK_ROPE = 64
V_DIM = 128
Q_LORA = 256
KV_LORA = 128
MLA_WIDTH = MLA_HEADS * V_DIM
MLA_SCALE = (QK_NOPE + QK_ROPE) ** -0.5
LOG2_E = 1.4426950408889634
ROPE_BASE = 10000.0
NORM_EPS = 1e-6

LANES = 128
MLA_QK_PAD = 256
V_EXT = V_DIM + 16
PROJ_TILE = 512
SB_TILE = 256
SB_GROUP = 2
MLA_TILE = 512
POST_TILE = 512
VMEM_LIMIT = 56 * 1024 * 1024
SB_DEAD_MASS = 158.0
SB_MASKED_SCORE = -1e30
MLA_MASK_BIAS = -1e30

_F32 = jnp.float32
_BF16 = jnp.bfloat16


def _dot(a, b):
    return jnp.dot(a, b, preferred_element_type=_F32)


def _dot_nt(a, b):
    return lax.dot_general(a, b, (((1,), (1,)), ((), ())), preferred_element_type=_F32)


def _rms(x, g):
    y = x * lax.rsqrt(jnp.mean(x * x, axis=-1, keepdims=True) + NORM_EPS)
    return y * g


def _proj_kernel(x_ref, pos_ref, invf_ref, g_attn_ref, w1_ref, wvt_ref, wkrt_ref,
                 g_qa_ref, g_kva_ref, w2t_ref, wkn_ref, wvmt_ref,
                 qsb_ref, ksb_ref, vtsb_ref, qtcat_ref, kcat_ref, vmt_ref):
    u = _rms(x_ref[0], g_attn_ref[...]).astype(_BF16)
    p1 = _dot(u, w1_ref[...])
    qsb_ref[0] = (p1[:, :SB_WIDTH] * (SB_SCALE * LOG2_E)).astype(_BF16)
    ksb_ref[0] = p1[:, SB_WIDTH:2 * SB_WIDTH].astype(_BF16)
    vtsb_ref[0] = _dot_nt(wvt_ref[...], u).astype(_BF16)

    o = 2 * SB_WIDTH
    cqn = _rms(p1[:, o:o + Q_LORA], g_qa_ref[...]).astype(_BF16)
    o += Q_LORA
    ckvn = _rms(p1[:, o:o + KV_LORA], g_kva_ref[...]).astype(_BF16)

    ang = invf_ref[...] * pos_ref[0].astype(_F32)
    cos_h = jnp.cos(ang)
    sin_h = jnp.sin(ang)
    reps = LANES // QK_ROPE
    cost = jnp.concatenate([cos_h, cos_h] * reps, axis=0)
    sint = jnp.concatenate([-sin_h, sin_h] * reps, axis=0)

    krt = _dot_nt(wkrt_ref[...], u)
    kr_roped = (krt[:LANES] * cost + krt[LANES:] * sint).T.astype(_BF16)

    p2t = _dot_nt(w2t_ref[...], cqn)
    kn = _dot(ckvn, wkn_ref[...]).astype(_BF16)
    hw = MLA_HEADS * LANES
    for h in range(MLA_HEADS):
        sl = slice(h * LANES, (h + 1) * LANES)
        qn = p2t[sl] * (MLA_SCALE * LOG2_E)
        qr = (p2t[hw + h * LANES:hw + (h + 1) * LANES] * cost
              + p2t[2 * hw + h * LANES:2 * hw + (h + 1) * LANES] * sint) * (MLA_SCALE * LOG2_E)
        qtcat_ref[0, h, :LANES, :] = qn.astype(_BF16)
        qtcat_ref[0, h, LANES:, :] = qr.astype(_BF16)
        kcat_ref[0, :, h * MLA_QK_PAD:h * MLA_QK_PAD + LANES] = kn[:, sl]
        kcat_ref[0, :, h * MLA_QK_PAD + LANES:(h + 1) * MLA_QK_PAD] = kr_roped
    vt = _dot_nt(wvmt_ref[...], ckvn).astype(_BF16)
    ts = vt.shape[1]
    ones_row = (lax.broadcasted_iota(jnp.int32, (V_EXT - V_DIM, ts), 0) == 0).astype(_BF16)
    for h in range(MLA_HEADS):
        vmt_ref[0, h, :V_DIM, :] = vt[h * V_DIM:(h + 1) * V_DIM]
        vmt_ref[0, h, V_DIM:, :] = ones_row


def _full(shape):
    return pl.BlockSpec(shape, lambda *_: (0,) * len(shape))


def _proj(x, pos_row, invf, g_attn, w1, wvt, wkrt, g_qa, g_kva, w2t, wkn, wvmt):
    bsz, seq, d = x.shape
    ts = PROJ_TILE
    grid = (bsz, seq // ts)
    row = lambda b, s: (b, s, 0)
    col = lambda b, s: (b, 0, s)
    out_shape = (
        jax.ShapeDtypeStruct((bsz, seq, SB_WIDTH), _BF16),
        jax.ShapeDtypeStruct((bsz, seq, SB_WIDTH), _BF16),
        jax.ShapeDtypeStruct((bsz, SB_WIDTH, seq), _BF16),
        jax.ShapeDtypeStruct((bsz, MLA_HEADS, MLA_QK_PAD, seq), _BF16),
        jax.ShapeDtypeStruct((bsz, seq, MLA_HEADS * MLA_QK_PAD), _BF16),
        jax.ShapeDtypeStruct((bsz, MLA_HEADS, V_EXT, seq), _BF16),
    )
    heads_col = lambda b, s: (b, 0, 0, s)
    return pl.pallas_call(
        _proj_kernel,
        grid=grid,
        in_specs=[
            pl.BlockSpec((1, ts, d), row),
            pl.BlockSpec((1, 1, ts), col),
            _full(invf.shape), _full(g_attn.shape), _full(w1.shape),
            _full(wvt.shape), _full(wkrt.shape), _full(g_qa.shape), _full(g_kva.shape),
            _full(w2t.shape), _full(wkn.shape), _full(wvmt.shape),
        ],
        out_specs=(
            pl.BlockSpec((1, ts, SB_WIDTH), row),
            pl.BlockSpec((1, ts, SB_WIDTH), row),
            pl.BlockSpec((1, SB_WIDTH, ts), col),
            pl.BlockSpec((1, MLA_HEADS, MLA_QK_PAD, ts), heads_col),
            pl.BlockSpec((1, ts, MLA_HEADS * MLA_QK_PAD), row),
            pl.BlockSpec((1, MLA_HEADS, V_EXT, ts), heads_col),
        ),
        out_shape=out_shape,
        compiler_params=pltpu.CompilerParams(
            dimension_semantics=("arbitrary", "arbitrary"), vmem_limit_bytes=VMEM_LIMIT),
        name="proj",
    )(x, pos_row, invf, g_attn, w1, wvt, wkrt, g_qa, g_kva, w2t, wkn, wvmt)


def _sb_kernel(q_ref, k_ref, vt_ref, tri_ref, o_ref, acc_ref, mass_ref, nls_ref, spb_ref,
               ab_ref):
    t = SB_TILE
    n_q = q_ref.shape[1] // t
    tri = tri_ref[...]
    sub = lax.broadcasted_iota(jnp.int32, (LANES, t), 0)
    first = sub < SB_HEAD_DIM

    def strict_mask():
        key = lax.broadcasted_iota(jnp.int32, (t, t), 0)
        qry = lax.broadcasted_iota(jnp.int32, (t, t), 1)
        return key < qry

    def stage_softplus(c, qt, j, masked):
        start = pl.multiple_of(j * t, t)
        z = _dot(k_ref[0, pl.ds(start, t), :], qt)
        if masked:
            z = jnp.where(strict_mask(), z, SB_MASKED_SCORE)
        zb = z.astype(_BF16)
        pos = jnp.maximum(zb, 0.0)
        neg = zb - pos
        lg = jnp.log2((1.0 + jnp.exp2(neg - pos)).astype(_F32)).astype(_BF16)
        sp = pos + lg
        spb_ref[c] = sp
        nls_ref[c] = neg - lg
        return sp[:1, :].astype(_F32)

    def stage_weights(c, sp_row0):
        excl = _dot(tri, spb_ref[c])
        ab_ref[c] = jnp.exp2(nls_ref[c] - excl.astype(_BF16))
        return excl[:1, :] + sp_row0

    def stage_pv(c, j):
        start = pl.multiple_of(j * t, t)
        return _dot(vt_ref[0, :, pl.ds(start, t)], ab_ref[c])

    def load_qt(i):
        start = pl.multiple_of(i * t, t)
        qt = q_ref[0, pl.ds(start, t), :].astype(_F32).T
        zero = jnp.zeros_like(qt)
        return (jnp.where(first, qt, zero).astype(_BF16), jnp.where(first, zero, qt).astype(_BF16))

    def group(g, carry):
        chains = []
        for u in range(SB_GROUP):
            i = g * SB_GROUP + u
            qts = load_qt(i)
            for hh in range(2):
                chains.append((qts[hh], i, True))
                chains.append((qts[hh], jnp.maximum(i - 1, 0), False))
        rows0 = [stage_softplus(c, *chain) for c, chain in enumerate(chains)]
        tots = [stage_weights(c, rows0[c]) for c in range(len(chains))]
        pvs = [stage_pv(c, j) for c, (_, j, _) in enumerate(chains)]
        for u in range(SB_GROUP):
            has_prev = jnp.where(g * SB_GROUP + u > 0, 1.0, 0.0).astype(_F32)
            for hh in range(2):
                d = 4 * u + 2 * hh
                acc_ref[u, hh] = pvs[d] + (jnp.exp2(-tots[d]) * has_prev) * pvs[d + 1]
                mass_ref[u, hh] = tots[d] + has_prev * tots[d + 1]
        for u in range(SB_GROUP):
            i = g * SB_GROUP + u

            def min_mass():
                return jnp.min(jnp.minimum(mass_ref[u, 0], mass_ref[u, 1]))

            def cond(c):
                j, mm = c
                return jnp.logical_and(j >= 0, mm < SB_DEAD_MASS)

            def body(c):
                j, _ = c
                qts = load_qt(i)
                for hh in range(2):
                    tot = stage_weights(hh, stage_softplus(hh, qts[hh], j, False))
                    mass = mass_ref[u, hh]
                    acc_ref[u, hh] += jnp.exp2(-mass) * stage_pv(hh, j)
                    mass_ref[u, hh] = mass + tot
                return j - 1, min_mass()

            lax.while_loop(cond, body, (i - 2, min_mass()))
            out_t = jnp.where(first, acc_ref[u, 0], acc_ref[u, 1])
            o_ref[0, pl.ds(pl.multiple_of(i * t, t), t), :] = out_t.T
        return carry

    lax.fori_loop(0, n_q // SB_GROUP, group, 0)


def _sb_attention(qsb, ksb, vtsb, tri):
    bsz, seq, _ = qsb.shape
    t = SB_TILE
    grid = (bsz, SB_WIDTH // LANES)
    return pl.pallas_call(
        _sb_kernel,
        grid=grid,
        in_specs=[
            pl.BlockSpec((1, seq, LANES), lambda b, p: (b, 0, p)),
            pl.BlockSpec((1, seq, LANES), lambda b, p: (b, 0, p)),
            pl.BlockSpec((1, LANES, seq), lambda b, p: (b, p, 0)),
            pl.BlockSpec((t, t), lambda b, p: (0, 0)),
        ],
        out_specs=pl.BlockSpec((1, seq, LANES), lambda b, p: (b, 0, p)),
        out_shape=jax.ShapeDtypeStruct((bsz, seq, SB_WIDTH), _F32),
        scratch_shapes=[pltpu.VMEM((SB_GROUP, 2, LANES, t), _F32),
                        pltpu.VMEM((SB_GROUP, 2, 1, t), _F32),
                        pltpu.VMEM((4 * SB_GROUP, t, t), _BF16),
                        pltpu.VMEM((4 * SB_GROUP, t, t), _BF16),
                        pltpu.VMEM((4 * SB_GROUP, t, t), _BF16)],
        compiler_params=pltpu.CompilerParams(
            dimension_semantics=("arbitrary", "arbitrary"), vmem_limit_bytes=VMEM_LIMIT),
        name="sb_attn",
    )(qsb, ksb, vtsb, tri)


def _mla_kernel(qt_ref, k_ref, vt_ref, bias_ref, o_ref, acc_ref, m_ref, sa_ref, sb_ref):
    t = MLA_TILE
    n_q = k_ref.shape[1] // t
    n_off = n_q * (n_q - 1) // 2

    def scores_into(dst_ref, i, j):
        i = jnp.minimum(i, n_q - 1)
        qs = pl.multiple_of(i * t, t)
        ks = pl.multiple_of(j * t, t)
        dst_ref[...] = _dot(k_ref[0, pl.ds(ks, t), :], qt_ref[0, 0, :, pl.ds(qs, t)])

    def update(src_ref, i, j, diagonal):
        ks = pl.multiple_of(j * t, t)
        vt = vt_ref[0, 0, :, pl.ds(ks, t)]
        s = src_ref[...]
        if diagonal:
            s = s + bias_ref[...]
        m_old = m_ref[i]
        m_new = jnp.maximum(m_old, jnp.max(s, axis=0, keepdims=True))
        p = jnp.exp2(s - m_new)
        alpha = jnp.exp2(m_old - m_new)
        acc = alpha * acc_ref[i] + _dot(vt, p.astype(_BF16))
        if diagonal:
            qs = pl.multiple_of(i * t, t)
            o_ref[0, pl.ds(qs, t), :] = (acc[:V_DIM, :] / acc[V_DIM:V_DIM + 1, :]).T
        else:
            acc_ref[i] = acc
            m_ref[i] = m_new

    acc_ref[...] = jnp.zeros_like(acc_ref)
    m_ref[...] = jnp.full_like(m_ref, jnp.finfo(_F32).min)

    def advance(i, j):
        wrap = j + 1 >= i
        return jnp.where(wrap, i + 1, i), jnp.where(wrap, 0, j + 1)

    def off_pair(_, carry):
        i, j = carry
        i1, j1 = advance(i, j)
        scores_into(sb_ref, i1, j1)
        update(sa_ref, i, j, False)
        i2, j2 = advance(i1, j1)
        scores_into(sa_ref, i2, j2)
        update(sb_ref, i1, j1, False)
        return i2, j2

    if n_off:
        scores_into(sa_ref, 1, 0)
        i, j = lax.fori_loop(0, n_off // 2, off_pair, (jnp.int32(1), jnp.int32(0)))
        if n_off % 2:
            update(sa_ref, i, j, False)

    def diag_pair(ii, carry):
        i = 2 * ii
        scores_into(sb_ref, i + 1, i + 1)
        update(sa_ref, i, i, True)
        scores_into(sa_ref, i + 2, jnp.minimum(i + 2, n_q - 1))
        update(sb_ref, i + 1, i + 1, True)
        return carry

    scores_into(sa_ref, 0, 0)
    lax.fori_loop(0, n_q // 2, diag_pair, 0)
    if n_q % 2:
        update(sa_ref, n_q - 1, n_q - 1, True)


def _mla_attention(qtcat, kcat, vmt, bias):
    bsz, seq, _ = kcat.shape
    t = MLA_TILE
    grid = (bsz, MLA_HEADS)
    return pl.pallas_call(
        _mla_kernel,
        grid=grid,
        in_specs=[
            pl.BlockSpec((1, 1, MLA_QK_PAD, seq), lambda b, h: (b, h, 0, 0)),
            pl.BlockSpec((1, seq, MLA_QK_PAD), lambda b, h: (b, 0, h)),
            pl.BlockSpec((1, 1, V_EXT, seq), lambda b, h: (b, h, 0, 0)),
            pl.BlockSpec((t, t), lambda b, h: (0, 0)),
        ],
        out_specs=pl.BlockSpec((1, seq, V_DIM), lambda b, h: (b, 0, h)),
        out_shape=jax.ShapeDtypeStruct((bsz, seq, MLA_WIDTH), _F32),
        scratch_shapes=[pltpu.VMEM((seq // t, V_EXT, t), _F32), pltpu.VMEM((seq // t, 1, t), _F32),
                        pltpu.VMEM((t, t), _F32), pltpu.VMEM((t, t), _F32)],
        compiler_params=pltpu.CompilerParams(
            dimension_semantics=("arbitrary", "arbitrary"), vmem_limit_bytes=VMEM_LIMIT),
        name="mla_attn",
    )(qtcat, kcat, vmt, bias)


def _post_kernel(x_ref, osb_ref, omla_ref, g_sb_ref, g_mla_ref, wo_ref, g_mlp_ref, wup_ref,
                 wdown_ref, g_out_ref, out_ref, *, apply_out_norm):
    ms = _rms(osb_ref[...], g_sb_ref[...]).astype(_BF16)
    mm = _rms(omla_ref[...], g_mla_ref[...]).astype(_BF16)
    h1 = x_ref[...] + _dot(jnp.concatenate([ms, mm], axis=1), wo_ref[...])
    v = _rms(h1, g_mlp_ref[...]).astype(_BF16)
    hid = jnp.square(jnp.maximum(_dot(v, wup_ref[...]), 0.0)).astype(_BF16)
    h2 = h1 + _dot(hid, wdown_ref[...])
    out_ref[...] = _rms(h2, g_out_ref[...]) if apply_out_norm else h2


def _post(x2, osb2, omla2, g_sb, g_mla, wo, g_mlp, wup, wdown, g_out, apply_out_norm):
    n, d = x2.shape
    tm = POST_TILE
    once = pl.Buffered(1)
    rows = lambda w: pl.BlockSpec((tm, w), lambda r: (r, 0))
    const = lambda a: pl.BlockSpec(a.shape, lambda r: (0, 0), pipeline_mode=once)
    return pl.pallas_call(
        functools.partial(_post_kernel, apply_out_norm=apply_out_norm),
        grid=(n // tm,),
        in_specs=[rows(d), rows(SB_WIDTH), rows(MLA_WIDTH), const(g_sb), const(g_mla), const(wo),
                  const(g_mlp), const(wup), const(wdown), const(g_out)],
        out_specs=rows(d),
        out_shape=jax.ShapeDtypeStruct((n, d), _F32),
        compiler_params=pltpu.CompilerParams(
            dimension_semantics=("arbitrary",), vmem_limit_bytes=VMEM_LIMIT),
        name="post",
    )(x2, osb2, omla2, g_sb, g_mla, wo, g_mlp, wup, wdown, g_out)


def _swap_halves(w):
    half = QK_ROPE // 2
    return jnp.concatenate([w[..., half:], w[..., :half]], axis=-1)


def _layer_weights(w_in, w_q_b, w_kv_b):
    d = w_in.shape[0]
    s2, s3 = 2 * SB_WIDTH, 3 * SB_WIDTH
    s5 = s3 + Q_LORA + KV_LORA
    w1 = jnp.concatenate([w_in[:, :s2], w_in[:, s3:s5]], axis=1).astype(_BF16)
    wvt = w_in[:, s2:s3].T.astype(_BF16)
    wkr = w_in[:, s5:]
    pad = jnp.zeros((d, LANES - QK_ROPE), w_in.dtype)
    wkrt = jnp.concatenate([wkr, pad, _swap_halves(wkr), pad], axis=1).T.astype(_BF16)
    wqb = w_q_b.reshape(Q_LORA, MLA_HEADS, QK_NOPE + QK_ROPE)
    qn = wqb[:, :, :QK_NOPE].reshape(Q_LORA, MLA_HEADS * QK_NOPE)
    qr = wqb[:, :, QK_NOPE:]
    qpad = jnp.zeros((Q_LORA, MLA_HEADS, LANES - QK_ROPE), w_q_b.dtype)
    qr_pad = jnp.concatenate([qr, qpad], axis=-1).reshape(Q_LORA, MLA_HEADS * LANES)
    qrr_pad = jnp.concatenate([_swap_halves(qr), qpad], axis=-1).reshape(Q_LORA, MLA_HEADS * LANES)
    w2t = jnp.concatenate([qn, qr_pad, qrr_pad], axis=1).T.astype(_BF16)
    wkvb = w_kv_b.reshape(KV_LORA, MLA_HEADS, QK_NOPE + V_DIM)
    wkn = wkvb[:, :, :QK_NOPE].reshape(KV_LORA, MLA_HEADS * QK_NOPE).astype(_BF16)
    wvmt = wkvb[:, :, QK_NOPE:].reshape(KV_LORA, MLA_HEADS * V_DIM).T.astype(_BF16)
    return w1, wvt, wkrt, w2t, wkn, wvmt


def kernel(x, positions, attn_norm_g, w_in, q_a_norm_g, w_q_b, kv_a_norm_g, w_kv_b, sb_out_norm_g,
           mla_out_norm_g, w_o, mlp_norm_g, w_up, w_down, final_norm_g):
    bsz, seq, d = x.shape
    depth = w_in.shape[0]
    assert seq % PROJ_TILE == 0 and seq % (SB_TILE * SB_GROUP) == 0 and seq % MLA_TILE == 0
    assert (bsz * seq) % POST_TILE == 0

    half = QK_ROPE // 2
    inv_freq = ROPE_BASE ** (-jnp.arange(half, dtype=_F32) / half)
    invf = inv_freq[:, None]
    pos_row = positions[:, None, :]
    r = lax.broadcasted_iota(jnp.int32, (SB_TILE, SB_TILE), 0)
    c = lax.broadcasted_iota(jnp.int32, (SB_TILE, SB_TILE), 1)
    tri = (c > r).astype(_BF16)
    key = lax.broadcasted_iota(jnp.int32, (MLA_TILE, MLA_TILE), 0)
    qry = lax.broadcasted_iota(jnp.int32, (MLA_TILE, MLA_TILE), 1)
    bias = jnp.where(key <= qry, 0.0, MLA_MASK_BIAS).astype(_F32)

    row = lambda g: g[None, :]
    h = x
    for l in range(depth):
        w1, wvt, wkrt, w2t, wkn, wvmt = _layer_weights(w_in[l], w_q_b[l], w_kv_b[l])
        qsb, ksb, vtsb, qtcat, kcat, vmt = _proj(
            h, pos_row, invf, row(attn_norm_g[l]), w1, wvt, wkrt, row(q_a_norm_g[l]),
            row(kv_a_norm_g[l]), w2t, wkn, wvmt)
        osb = _sb_attention(qsb, ksb, vtsb, tri)
        omla = _mla_attention(qtcat, kcat, vmt, bias)
        last = l == depth - 1
        h = _post(h.reshape(bsz * seq, d), osb.reshape(bsz * seq, SB_WIDTH),
                  omla.reshape(bsz * seq, MLA_WIDTH), row(sb_out_norm_g[l]),
                  row(mla_out_norm_g[l]), w_o[l].astype(_BF16), row(mlp_norm_g[l]),
                  w_up[l].astype(_BF16), w_down[l].astype(_BF16), row(final_norm_g),
                  last).reshape(bsz, seq, d)
    return h
```

```python
import functools

import jax
import jax.numpy as jnp
from jax import lax
from jax.experimental import pallas as pl
from jax.experimental.pallas import tpu as pltpu

SB_HEADS = 8
SB_HEAD_DIM = 64
SB_WIDTH = SB_HEADS * SB_HEAD_DIM
SB_SCALE = SB_HEAD_DIM ** -0.5
MLA_HEADS = 4
QK_NOPE = 128
QK_ROPE = 64
V_DIM = 128
Q_LORA = 256
KV_LORA = 128
MLA_WIDTH = MLA_HEADS * V_DIM
MLA_SCALE = (QK_NOPE + QK_ROPE) ** -0.5
LOG2_E = 1.4426950408889634
ROPE_BASE = 10000.0
NORM_EPS = 1e-6

LANES = 128
MLA_QK_PAD = 256
V_EXT = V_DIM + 16
PROJ_TILE = 512
SB_TILE = 256
MLA_TILE = 512
POST_TILE = 512
VMEM_LIMIT = 56 * 1024 * 1024
SB_DEAD_MASS = 158.0
SB_MASKED_SCORE = -1e30
MLA_MASK_BIAS = -1e30

_F32 = jnp.float32
_BF16 = jnp.bfloat16


def _dot(a, b):
    return jnp.dot(a, b, preferred_element_type=_F32)


def _dot_nt(a, b):
    return lax.dot_general(a, b, (((1,), (1,)), ((), ())), preferred_element_type=_F32)


def _rms(x, g):
    y = x * lax.rsqrt(jnp.mean(x * x, axis=-1, keepdims=True) + NORM_EPS)
    return y * g


def _proj_kernel(x_ref, pos_ref, invf_ref, g_attn_ref, w1_ref, wvt_ref, wkrt_ref,
                 g_qa_ref, g_kva_ref, w2t_ref, wkn_ref, wvmt_ref,
                 qsb_ref, ksb_ref, vtsb_ref, qtcat_ref, kcat_ref, vmt_ref):
    u = _rms(x_ref[0], g_attn_ref[...]).astype(_BF16)
    p1 = _dot(u, w1_ref[...])
    qsb_ref[0] = (p1[:, :SB_WIDTH] * (SB_SCALE * LOG2_E)).astype(_BF16)
    ksb_ref[0] = p1[:, SB_WIDTH:2 * SB_WIDTH].astype(_BF16)
    vtsb_ref[0] = _dot_nt(wvt_ref[...], u).astype(_BF16)

    o = 2 * SB_WIDTH
    cqn = _rms(p1[:, o:o + Q_LORA], g_qa_ref[...]).astype(_BF16)
    o += Q_LORA
    ckvn = _rms(p1[:, o:o + KV_LORA], g_kva_ref[...]).astype(_BF16)

    ang = invf_ref[...] * pos_ref[0].astype(_F32)
    cos_h = jnp.cos(ang)
    sin_h = jnp.sin(ang)
    reps = LANES // QK_ROPE
    cost = jnp.concatenate([cos_h, cos_h] * reps, axis=0)
    sint = jnp.concatenate([-sin_h, sin_h] * reps, axis=0)

    krt = _dot_nt(wkrt_ref[...], u)
    kr_roped = (krt[:LANES] * cost + krt[LANES:] * sint).T.astype(_BF16)

    p2t = _dot_nt(w2t_ref[...], cqn)
    kn = _dot(ckvn, wkn_ref[...]).astype(_BF16)
    hw = MLA_HEADS * LANES
    for h in range(MLA_HEADS):
        sl = slice(h * LANES, (h + 1) * LANES)
        qn = p2t[sl] * (MLA_SCALE * LOG2_E)
        qr = (p2t[hw + h * LANES:hw + (h + 1) * LANES] * cost
              + p2t[2 * hw + h * LANES:2 * hw + (h + 1) * LANES] * sint) * (MLA_SCALE * LOG2_E)
        qtcat_ref[0, h, :LANES, :] = qn.astype(_BF16)
        qtcat_ref[0, h, LANES:, :] = qr.astype(_BF16)
        kcat_ref[0, :, h * MLA_QK_PAD:h * MLA_QK_PAD + LANES] = kn[:, sl]
        kcat_ref[0, :, h * MLA_QK_PAD + LANES:(h + 1) * MLA_QK_PAD] = kr_roped
    vt = _dot_nt(wvmt_ref[...], ckvn).astype(_BF16)
    ts = vt.shape[1]
    ones_row = (lax.broadcasted_iota(jnp.int32, (V_EXT - V_DIM, ts), 0) == 0).astype(_BF16)
    for h in range(MLA_HEADS):
        vmt_ref[0, h, :V_DIM, :] = vt[h * V_DIM:(h + 1) * V_DIM]
        vmt_ref[0, h, V_DIM:, :] = ones_row


def _full(shape):
    return pl.BlockSpec(shape, lambda *_: (0,) * len(shape))


def _proj(x, pos_row, invf, g_attn, w1, wvt, wkrt, g_qa, g_kva, w2t, wkn, wvmt):
    bsz, seq, d = x.shape
    ts = PROJ_TILE
    grid = (bsz, seq // ts)
    row = lambda b, s: (b, s, 0)
    col = lambda b, s: (b, 0, s)
    out_shape = (
        jax.ShapeDtypeStruct((bsz, seq, SB_WIDTH), _BF16),
        jax.ShapeDtypeStruct((bsz, seq, SB_WIDTH), _BF16),
        jax.ShapeDtypeStruct((bsz, SB_WIDTH, seq), _BF16),
        jax.ShapeDtypeStruct((bsz, MLA_HEADS, MLA_QK_PAD, seq), _BF16),
        jax.ShapeDtypeStruct((bsz, seq, MLA_HEADS * MLA_QK_PAD), _BF16),
        jax.ShapeDtypeStruct((bsz, MLA_HEADS, V_EXT, seq), _BF16),
    )
    heads_col = lambda b, s: (b, 0, 0, s)
    return pl.pallas_call(
        _proj_kernel,
        grid=grid,
        in_specs=[
            pl.BlockSpec((1, ts, d), row),
            pl.BlockSpec((1, 1, ts), col),
            _full(invf.shape), _full(g_attn.shape), _full(w1.shape),
            _full(wvt.shape), _full(wkrt.shape), _full(g_qa.shape), _full(g_kva.shape),
            _full(w2t.shape), _full(wkn.shape), _full(wvmt.shape),
        ],
        out_specs=(
            pl.BlockSpec((1, ts, SB_WIDTH), row),
            pl.BlockSpec((1, ts, SB_WIDTH), row),
            pl.BlockSpec((1, SB_WIDTH, ts), col),
            pl.BlockSpec((1, MLA_HEADS, MLA_QK_PAD, ts), heads_col),
            pl.BlockSpec((1, ts, MLA_HEADS * MLA_QK_PAD), row),
            pl.BlockSpec((1, MLA_HEADS, V_EXT, ts), heads_col),
        ),
        out_shape=out_shape,
        compiler_params=pltpu.CompilerParams(
            dimension_semantics=("arbitrary", "arbitrary"), vmem_limit_bytes=VMEM_LIMIT),
        name="proj",
    )(x, pos_row, invf, g_attn, w1, wvt, wkrt, g_qa, g_kva, w2t, wkn, wvmt)


def _sb_parts(q_ref, k_ref, vt_ref, tri_ref, o_ref, acc_ref, mass_ref, nls_ref, spb_ref, ab_ref):
    t = SB_TILE
    tri = tri_ref[...]
    sub = lax.broadcasted_iota(jnp.int32, (LANES, t), 0)
    first = sub < SB_HEAD_DIM

    def strict_mask():
        key = lax.broadcasted_iota(jnp.int32, (t, t), 0)
        qry = lax.broadcasted_iota(jnp.int32, (t, t), 1)
        return key < qry

    def stage_softplus(c, qt, j, masked):
        start = pl.multiple_of(j * t, t)
        z = _dot(k_ref[0, pl.ds(start, t), :], qt)
        if masked:
            z = jnp.where(strict_mask(), z, SB_MASKED_SCORE)
        zb = z.astype(_BF16)
        pos = jnp.maximum(zb, 0.0)
        neg = zb - pos
        lg = jnp.log2((1.0 + jnp.exp2(neg - pos)).astype(_F32)).astype(_BF16)
        sp = pos + lg
        spb_ref[c] = sp
        nls_ref[c] = neg - lg
        return sp[:1, :].astype(_F32)

    def stage_weights(c, sp_row0):
        excl = _dot(tri, spb_ref[c])
        ab_ref[c] = jnp.exp2(nls_ref[c] - excl.astype(_BF16))
        return excl[:1, :] + sp_row0

    def stage_pv(c, j):
        start = pl.multiple_of(j * t, t)
        return _dot(vt_ref[0, :, pl.ds(start, t)], ab_ref[c])

    def load_qt(i):
        start = pl.multiple_of(i * t, t)
        qt = q_ref[0, pl.ds(start, t), :].astype(_F32).T
        zero = jnp.zeros_like(qt)
        return (jnp.where(first, qt, zero).astype(_BF16), jnp.where(first, zero, qt).astype(_BF16))

    def straight_line(i):
        state = {}

        def begin():
            qts = load_qt(i)
            prev = jnp.maximum(i - 1, 0)
            state["chains"] = [(qts[0], i, True), (qts[0], prev, False),
                               (qts[1], i, True), (qts[1], prev, False)]
            state["rows0"], state["tots"] = [], []

        def softplus_step(c):
            return lambda: state["rows0"].append(stage_softplus(c, *state["chains"][c]))

        def weights_step(c):
            return lambda: state["tots"].append(stage_weights(c, state["rows0"][c]))

        def combine():
            tots = state["tots"]
            pvs = [stage_pv(c, j) for c, (_, j, _) in enumerate(state["chains"])]
            has_prev = jnp.where(i > 0, 1.0, 0.0).astype(_F32)
            for hh in range(2):
                d = 2 * hh
                acc_ref[hh] = pvs[d] + (jnp.exp2(-tots[d]) * has_prev) * pvs[d + 1]
                mass_ref[hh] = tots[d] + has_prev * tots[d + 1]

        return ([begin] + [softplus_step(c) for c in range(4)]
                + [weights_step(c) for c in range(4)] + [combine])

    def finish(i):
        def min_mass():
            return jnp.min(jnp.minimum(mass_ref[0], mass_ref[1]))

        def cond(c):
            j, mm = c
            return jnp.logical_and(j >= 0, mm < SB_DEAD_MASS)

        def body(c):
            j, _ = c
            qts = load_qt(i)
            for hh in range(2):
                tot = stage_weights(hh, stage_softplus(hh, qts[hh], j, False))
                mass = mass_ref[hh]
                acc_ref[hh] += jnp.exp2(-mass) * stage_pv(hh, j)
                mass_ref[hh] = mass + tot
            return j - 1, min_mass()

        lax.while_loop(cond, body, (i - 2, min_mass()))
        out_t = jnp.where(first, acc_ref[0], acc_ref[1])
        o_ref[0, pl.ds(pl.multiple_of(i * t, t), t), :] = out_t.T

    return straight_line, finish


def _mla_parts(qt_ref, k_ref, vt_ref, bias_ref, o_ref, acc_ref, m_ref):
    t = MLA_TILE
    n_q = k_ref.shape[1] // t

    def scores_into(dst_ref, i, j):
        i = jnp.minimum(i, n_q - 1)
        j = jnp.minimum(j, n_q - 1)
        qs = pl.multiple_of(i * t, t)
        ks = pl.multiple_of(j * t, t)
        dst_ref[...] = _dot(k_ref[0, pl.ds(ks, t), :], qt_ref[0, 0, :, pl.ds(qs, t)])

    def update(src_ref, i, j, diagonal):
        ks = pl.multiple_of(j * t, t)
        vt = vt_ref[0, 0, :, pl.ds(ks, t)]
        s = src_ref[...]
        if diagonal:
            s = s + bias_ref[...]
        m_old = m_ref[i]
        m_new = jnp.maximum(m_old, jnp.max(s, axis=0, keepdims=True))
        p = jnp.exp2(s - m_new)
        alpha = jnp.exp2(m_old - m_new)
        acc = alpha * acc_ref[i] + _dot(vt, p.astype(_BF16))
        if diagonal:
            qs = pl.multiple_of(i * t, t)
            o_ref[0, pl.ds(qs, t), :] = (acc[:V_DIM, :] / acc[V_DIM:V_DIM + 1, :]).T
        else:
            acc_ref[i] = acc
            m_ref[i] = m_new

    return scores_into, update


def _attn_kernel(qt_ref, km_ref, vtm_ref, bias_ref, qs_ref, ks_ref, vts_ref, tri_ref,
                 om_ref, os_ref, acc_ref, m_ref, sa_ref, sb_ref, sacc_ref, smass_ref, nls_ref,
                 spb_ref, ab_ref):
    scores_into, update = _mla_parts(qt_ref, km_ref, vtm_ref, bias_ref, om_ref, acc_ref, m_ref)
    sb_straight, sb_finish = _sb_parts(qs_ref, ks_ref, vts_ref, tri_ref, os_ref, sacc_ref,
                                       smass_ref, nls_ref, spb_ref, ab_ref)
    n_q = km_ref.shape[1] // MLA_TILE
    n_off = n_q * (n_q - 1) // 2
    n_sb = qs_ref.shape[1] // SB_TILE

    acc_ref[...] = jnp.zeros_like(acc_ref)
    m_ref[...] = jnp.full_like(m_ref, jnp.finfo(_F32).min)

    def advance(i, j):
        wrap = j + 1 >= i
        return jnp.where(wrap, i + 1, i), jnp.where(wrap, 0, j + 1)

    def off_steps(n, carry):
        i, j = carry
        i1, j1 = advance(i, j)
        i2, j2 = advance(i1, j1)
        return [lambda: scores_into(sb_ref, i1, j1), lambda: update(sa_ref, i, j, False),
                lambda: scores_into(sa_ref, i2, j2), lambda: update(sb_ref, i1, j1, False)], (i2, j2)

    def diag_steps(n, carry):
        i = 2 * n
        return [lambda: scores_into(sb_ref, i + 1, i + 1), lambda: update(sa_ref, i, i, True),
                lambda: scores_into(sa_ref, i + 2, i + 2),
                lambda: update(sb_ref, i + 1, i + 1, True)], carry

    def plain(mla_steps):
        def body(n, carry):
            steps, carry = mla_steps(n, carry)
            for step in steps:
                step()
            return carry
        return body

    def with_sb(first_sb_tile, mla_steps):
        def body(n, carry):
            mla, carry = mla_steps(n, carry)
            sbs = sb_straight(first_sb_tile + n)
            order = [sbs[0], mla[0], sbs[1], sbs[2], mla[1], sbs[3], sbs[4], mla[2], sbs[5], sbs[6],
                     mla[3], sbs[7], sbs[8], sbs[9]]
            for step in order:
                step()
            sb_finish(first_sb_tile + n)
            return carry
        return body

    sb_done = 0
    trips = n_off // 2
    if n_off:
        scores_into(sa_ref, 1, 0)
        fused = min(trips, n_sb - sb_done)
        carry = (jnp.int32(1), jnp.int32(0))
        carry = lax.fori_loop(0, fused, with_sb(sb_done, off_steps), carry)
        carry = lax.fori_loop(fused, trips, plain(off_steps), carry)
        sb_done += fused
        if n_off % 2:
            update(sa_ref, carry[0], carry[1], False)

    trips = n_q // 2
    scores_into(sa_ref, 0, 0)
    fused = min(trips, n_sb - sb_done)
    lax.fori_loop(0, fused, with_sb(sb_done, diag_steps), 0)
    lax.fori_loop(fused, trips, plain(diag_steps), 0)
    sb_done += fused
    if n_q % 2:
        update(sa_ref, n_q - 1, n_q - 1, True)

    def sb_only(n, carry):
        for step in sb_straight(n):
            step()
        sb_finish(n)
        return carry

    lax.fori_loop(sb_done, n_sb, sb_only, 0)


def _attention(qtcat, kcat, vmt, bias, qsb, ksb, vtsb, tri):
    bsz, seq, _ = kcat.shape
    assert MLA_HEADS == SB_WIDTH // LANES
    t, ts = MLA_TILE, SB_TILE
    head_rows = lambda b, h: (b, h, 0, 0)
    lane_block = lambda b, h: (b, 0, h)
    return pl.pallas_call(
        _attn_kernel,
        grid=(bsz, MLA_HEADS),
        in_specs=[
            pl.BlockSpec((1, 1, MLA_QK_PAD, seq), head_rows),
            pl.BlockSpec((1, seq, MLA_QK_PAD), lane_block),
            pl.BlockSpec((1, 1, V_EXT, seq), head_rows),
            pl.BlockSpec((t, t), lambda b, h: (0, 0)),
            pl.BlockSpec((1, seq, LANES), lane_block),
            pl.BlockSpec((1, seq, LANES), lane_block),
            pl.BlockSpec((1, LANES, seq), lambda b, h: (b, h, 0)),
            pl.BlockSpec((ts, ts), lambda b, h: (0, 0)),
        ],
        out_specs=(pl.BlockSpec((1, seq, V_DIM), lane_block),
                   pl.BlockSpec((1, seq, LANES), lane_block)),
        out_shape=(jax.ShapeDtypeStruct((bsz, seq, MLA_WIDTH), _F32),
                   jax.ShapeDtypeStruct((bsz, seq, SB_WIDTH), _F32)),
        scratch_shapes=[pltpu.VMEM((seq // t, V_EXT, t), _F32), pltpu.VMEM((seq // t, 1, t), _F32),
                        pltpu.VMEM((t, t), _F32), pltpu.VMEM((t, t), _F32),
                        pltpu.VMEM((2, LANES, ts), _F32), pltpu.VMEM((2, 1, ts), _F32),
                        pltpu.VMEM((4, ts, ts), _BF16), pltpu.VMEM((4, ts, ts), _BF16),
                        pltpu.VMEM((4, ts, ts), _BF16)],
        compiler_params=pltpu.CompilerParams(
            dimension_semantics=("arbitrary", "arbitrary"), vmem_limit_bytes=VMEM_LIMIT),
        name="attention",
    )(qtcat, kcat, vmt, bias, qsb, ksb, vtsb, tri)


def _post_kernel(x_ref, osb_ref, omla_ref, g_sb_ref, g_mla_ref, wo_ref, g_mlp_ref, wup_ref,
                 wdown_ref, g_out_ref, out_ref, *, apply_out_norm):
    ms = _rms(osb_ref[...], g_sb_ref[...]).astype(_BF16)
    mm = _rms(omla_ref[...], g_mla_ref[...]).astype(_BF16)
    h1 = x_ref[...] + _dot(jnp.concatenate([ms, mm], axis=1), wo_ref[...])
    v = _rms(h1, g_mlp_ref[...]).astype(_BF16)
    hid = jnp.square(jnp.maximum(_dot(v, wup_ref[...]), 0.0)).astype(_BF16)
    h2 = h1 + _dot(hid, wdown_ref[...])
    out_ref[...] = _rms(h2, g_out_ref[...]) if apply_out_norm else h2


def _post(x2, osb2, omla2, g_sb, g_mla, wo, g_mlp, wup, wdown, g_out, apply_out_norm):
    n, d = x2.shape
    tm = POST_TILE
    once = pl.Buffered(1)
    rows = lambda w: pl.BlockSpec((tm, w), lambda r: (r, 0))
    const = lambda a: pl.BlockSpec(a.shape, lambda r: (0, 0), pipeline_mode=once)
    return pl.pallas_call(
        functools.partial(_post_kernel, apply_out_norm=apply_out_norm),
        grid=(n // tm,),
        in_specs=[rows(d), rows(SB_WIDTH), rows(MLA_WIDTH), const(g_sb), const(g_mla), const(wo),
                  const(g_mlp), const(wup), const(wdown), const(g_out)],
        out_specs=rows(d),
        out_shape=jax.ShapeDtypeStruct((n, d), _F32),
        compiler_params=pltpu.CompilerParams(
            dimension_semantics=("arbitrary",), vmem_limit_bytes=VMEM_LIMIT),
        name="post",
    )(x2, osb2, omla2, g_sb, g_mla, wo, g_mlp, wup, wdown, g_out)


def _swap_halves(w):
    half = QK_ROPE // 2
    return jnp.concatenate([w[..., half:], w[..., :half]], axis=-1)


def _layer_weights(w_in, w_q_b, w_kv_b):
    d = w_in.shape[0]
    s2, s3 = 2 * SB_WIDTH, 3 * SB_WIDTH
    s5 = s3 + Q_LORA + KV_LORA
    w1 = jnp.concatenate([w_in[:, :s2], w_in[:, s3:s5]], axis=1).astype(_BF16)
    wvt = w_in[:, s2:s3].T.astype(_BF16)
    wkr = w_in[:, s5:]
    pad = jnp.zeros((d, LANES - QK_ROPE), w_in.dtype)
    wkrt = jnp.concatenate([wkr, pad, _swap_halves(wkr), pad], axis=1).T.astype(_BF16)
    wqb = w_q_b.reshape(Q_LORA, MLA_HEADS, QK_NOPE + QK_ROPE)
    qn = wqb[:, :, :QK_NOPE].reshape(Q_LORA, MLA_HEADS * QK_NOPE)
    qr = wqb[:, :, QK_NOPE:]
    qpad = jnp.zeros((Q_LORA, MLA_HEADS, LANES - QK_ROPE), w_q_b.dtype)
    qr_pad = jnp.concatenate([qr, qpad], axis=-1).reshape(Q_LORA, MLA_HEADS * LANES)
    qrr_pad = jnp.concatenate([_swap_halves(qr), qpad], axis=-1).reshape(Q_LORA, MLA_HEADS * LANES)
    w2t = jnp.concatenate([qn, qr_pad, qrr_pad], axis=1).T.astype(_BF16)
    wkvb = w_kv_b.reshape(KV_LORA, MLA_HEADS, QK_NOPE + V_DIM)
    wkn = wkvb[:, :, :QK_NOPE].reshape(KV_LORA, MLA_HEADS * QK_NOPE).astype(_BF16)
    wvmt = wkvb[:, :, QK_NOPE:].reshape(KV_LORA, MLA_HEADS * V_DIM).T.astype(_BF16)
    return w1, wvt, wkrt, w2t, wkn, wvmt


def kernel(x, positions, attn_norm_g, w_in, q_a_norm_g, w_q_b, kv_a_norm_g, w_kv_b, sb_out_norm_g,
           mla_out_norm_g, w_o, mlp_norm_g, w_up, w_down, final_norm_g):
    bsz, seq, d = x.shape
    depth = w_in.shape[0]
    assert seq % PROJ_TILE == 0 and seq % SB_TILE == 0 and seq % MLA_TILE == 0
    assert (bsz * seq) % POST_TILE == 0

    half = QK_ROPE // 2
    inv_freq = ROPE_BASE ** (-jnp.arange(half, dtype=_F32) / half)
    invf = inv_freq[:, None]
    pos_row = positions[:, None, :]
    r = lax.broadcasted_iota(jnp.int32, (SB_TILE, SB_TILE), 0)
    c = lax.broadcasted_iota(jnp.int32, (SB_TILE, SB_TILE), 1)
    tri = (c > r).astype(_BF16)
    key = lax.broadcasted_iota(jnp.int32, (MLA_TILE, MLA_TILE), 0)
    qry = lax.broadcasted_iota(jnp.int32, (MLA_TILE, MLA_TILE), 1)
    bias = jnp.where(key <= qry, 0.0, MLA_MASK_BIAS).astype(_F32)

    row = lambda g: g[None, :]
    h = x
    for l in range(depth):
        w1, wvt, wkrt, w2t, wkn, wvmt = _layer_weights(w_in[l], w_q_b[l], w_kv_b[l])
        qsb, ksb, vtsb, qtcat, kcat, vmt = _proj(
            h, pos_row, invf, row(attn_norm_g[l]), w1, wvt, wkrt, row(q_a_norm_g[l]),
            row(kv_a_norm_g[l]), w2t, wkn, wvmt)
        omla, osb = _attention(qtcat, kcat, vmt, bias, qsb, ksb, vtsb, tri)
        last = l == depth - 1
        h = _post(h.reshape(bsz * seq, d), osb.reshape(bsz * seq, SB_WIDTH),
                  omla.reshape(bsz * seq, MLA_WIDTH), row(sb_out_norm_g[l]),
                  row(mla_out_norm_g[l]), w_o[l].astype(_BF16), row(mlp_norm_g[l]),
                  w_up[l].astype(_BF16), w_down[l].astype(_BF16), row(final_norm_g),
                  last).reshape(bsz, seq, d)
    return h
```

```python
import functools

import jax
import jax.numpy as jnp
from jax import lax
from jax.experimental import pallas as pl
from jax.experimental.pallas import tpu as pltpu

SB_HEADS = 8
SB_HEAD_DIM = 64
SB_WIDTH = SB_HEADS * SB_HEAD_DIM
SB_SCALE = SB_HEAD_DIM ** -0.5
MLA_HEADS = 4
QK_NOPE = 128
QK_ROPE = 64
V_DIM = 128
Q_LORA = 256
KV_LORA = 128
MLA_WIDTH = MLA_HEADS * V_DIM
MLA_SCALE = (QK_NOPE + QK_ROPE) ** -0.5
LOG2_E = 1.4426950408889634
ROPE_BASE = 10000.0
NORM_EPS = 1e-6

LANES = 128
MLA_QK_PAD = 256
V_EXT = V_DIM + 16
PROJ_TILE = 512
SB_TILE = 256
MLA_TILE = 512
POST_TILE = 512
VMEM_LIMIT = 56 * 1024 * 1024
SB_DEAD_MASS = 158.0
SB_MASKED_SCORE = -1e30
MLA_MASK_BIAS = -1e30

_F32 = jnp.float32
_BF16 = jnp.bfloat16


def _dot(a, b):
    return jnp.dot(a, b, preferred_element_type=_F32)


def _dot_nt(a, b):
    return lax.dot_general(a, b, (((1,), (1,)), ((), ())), preferred_element_type=_F32)


def _rms(x, g):
    y = x * lax.rsqrt(jnp.mean(x * x, axis=-1, keepdims=True) + NORM_EPS)
    return y * g


def _proj_kernel(x_ref, pos_ref, invf_ref, g_attn_ref, w1_ref, wvt_ref, wkrt_ref,
                 g_qa_ref, g_kva_ref, w2t_ref, wkn_ref, wvmt_ref,
                 qsb_ref, ksb_ref, vtsb_ref, qtcat_ref, kcat_ref, vmt_ref):
    u = _rms(x_ref[0], g_attn_ref[...]).astype(_BF16)
    p1 = _dot(u, w1_ref[...])
    qsb_ref[0] = (p1[:, :SB_WIDTH] * (SB_SCALE * LOG2_E)).astype(_BF16)
    ksb_ref[0] = p1[:, SB_WIDTH:2 * SB_WIDTH].astype(_BF16)
    vtsb_ref[0] = _dot_nt(wvt_ref[...], u).astype(_BF16)

    o = 2 * SB_WIDTH
    cqn = _rms(p1[:, o:o + Q_LORA], g_qa_ref[...]).astype(_BF16)
    o += Q_LORA
    ckvn = _rms(p1[:, o:o + KV_LORA], g_kva_ref[...]).astype(_BF16)

    ang = invf_ref[...] * pos_ref[0].astype(_F32)
    cos_h = jnp.cos(ang)
    sin_h = jnp.sin(ang)
    reps = LANES // QK_ROPE
    cost = jnp.concatenate([cos_h, cos_h] * reps, axis=0)
    sint = jnp.concatenate([-sin_h, sin_h] * reps, axis=0)

    krt = _dot_nt(wkrt_ref[...], u)
    kr_roped = (krt[:LANES] * cost + krt[LANES:] * sint).T.astype(_BF16)

    p2t = _dot_nt(w2t_ref[...], cqn)
    kn = _dot(ckvn, wkn_ref[...]).astype(_BF16)
    hw = MLA_HEADS * LANES
    for h in range(MLA_HEADS):
        sl = slice(h * LANES, (h + 1) * LANES)
        qn = p2t[sl] * (MLA_SCALE * LOG2_E)
        qr = (p2t[hw + h * LANES:hw + (h + 1) * LANES] * cost
              + p2t[2 * hw + h * LANES:2 * hw + (h + 1) * LANES] * sint) * (MLA_SCALE * LOG2_E)
        qtcat_ref[0, h, :LANES, :] = qn.astype(_BF16)
        qtcat_ref[0, h, LANES:, :] = qr.astype(_BF16)
        kcat_ref[0, :, h * MLA_QK_PAD:h * MLA_QK_PAD + LANES] = kn[:, sl]
        kcat_ref[0, :, h * MLA_QK_PAD + LANES:(h + 1) * MLA_QK_PAD] = kr_roped
    vt = _dot_nt(wvmt_ref[...], ckvn).astype(_BF16)
    ts = vt.shape[1]
    ones_row = (lax.broadcasted_iota(jnp.int32, (V_EXT - V_DIM, ts), 0) == 0).astype(_BF16)
    for h in range(MLA_HEADS):
        vmt_ref[0, h, :V_DIM, :] = vt[h * V_DIM:(h + 1) * V_DIM]
        vmt_ref[0, h, V_DIM:, :] = ones_row


def _full(shape):
    return pl.BlockSpec(shape, lambda *_: (0,) * len(shape))


def _proj(x, pos_row, invf, g_attn, w1, wvt, wkrt, g_qa, g_kva, w2t, wkn, wvmt):
    bsz, seq, d = x.shape
    ts = PROJ_TILE
    grid = (bsz, seq // ts)
    row = lambda b, s: (b, s, 0)
    col = lambda b, s: (b, 0, s)
    out_shape = (
        jax.ShapeDtypeStruct((bsz, seq, SB_WIDTH), _BF16),
        jax.ShapeDtypeStruct((bsz, seq, SB_WIDTH), _BF16),
        jax.ShapeDtypeStruct((bsz, SB_WIDTH, seq), _BF16),
        jax.ShapeDtypeStruct((bsz, MLA_HEADS, MLA_QK_PAD, seq), _BF16),
        jax.ShapeDtypeStruct((bsz, seq, MLA_HEADS * MLA_QK_PAD), _BF16),
        jax.ShapeDtypeStruct((bsz, MLA_HEADS, V_EXT, seq), _BF16),
    )
    heads_col = lambda b, s: (b, 0, 0, s)
    return pl.pallas_call(
        _proj_kernel,
        grid=grid,
        in_specs=[
            pl.BlockSpec((1, ts, d), row),
            pl.BlockSpec((1, 1, ts), col),
            _full(invf.shape), _full(g_attn.shape), _full(w1.shape),
            _full(wvt.shape), _full(wkrt.shape), _full(g_qa.shape), _full(g_kva.shape),
            _full(w2t.shape), _full(wkn.shape), _full(wvmt.shape),
        ],
        out_specs=(
            pl.BlockSpec((1, ts, SB_WIDTH), row),
            pl.BlockSpec((1, ts, SB_WIDTH), row),
            pl.BlockSpec((1, SB_WIDTH, ts), col),
            pl.BlockSpec((1, MLA_HEADS, MLA_QK_PAD, ts), heads_col),
            pl.BlockSpec((1, ts, MLA_HEADS * MLA_QK_PAD), row),
            pl.BlockSpec((1, MLA_HEADS, V_EXT, ts), heads_col),
        ),
        out_shape=out_shape,
        compiler_params=pltpu.CompilerParams(
            dimension_semantics=("arbitrary", "arbitrary"), vmem_limit_bytes=VMEM_LIMIT),
        name="proj",
    )(x, pos_row, invf, g_attn, w1, wvt, wkrt, g_qa, g_kva, w2t, wkn, wvmt)


def _sb_parts(q_ref, k_ref, vt_ref, tri_ref, o_ref, acc_ref, mass_ref, nls_ref, spb_ref, ab_ref):
    t = SB_TILE
    tri = tri_ref[...]
    sub = lax.broadcasted_iota(jnp.int32, (LANES, t), 0)
    first = sub < SB_HEAD_DIM

    def strict_mask():
        key = lax.broadcasted_iota(jnp.int32, (t, t), 0)
        qry = lax.broadcasted_iota(jnp.int32, (t, t), 1)
        return key < qry

    def stage_softplus(c, qt, j, masked):
        start = pl.multiple_of(j * t, t)
        z = _dot(k_ref[0, pl.ds(start, t), :], qt)
        if masked:
            z = jnp.where(strict_mask(), z, SB_MASKED_SCORE)
        zb = z.astype(_BF16)
        pos = jnp.maximum(zb, 0.0)
        neg = zb - pos
        lg = jnp.log2((1.0 + jnp.exp2(neg - pos)).astype(_F32)).astype(_BF16)
        sp = pos + lg
        spb_ref[c] = sp
        nls_ref[c] = neg - lg
        return sp[:1, :].astype(_F32)

    def stage_weights(c, sp_row0):
        excl = _dot(tri, spb_ref[c])
        ab_ref[c] = jnp.exp2(nls_ref[c] - excl.astype(_BF16))
        return excl[:1, :] + sp_row0

    def stage_pv(c, j):
        start = pl.multiple_of(j * t, t)
        return _dot(vt_ref[0, :, pl.ds(start, t)], ab_ref[c])

    def load_qt(i):
        start = pl.multiple_of(i * t, t)
        qt = q_ref[0, pl.ds(start, t), :].astype(_F32).T
        zero = jnp.zeros_like(qt)
        return (jnp.where(first, qt, zero).astype(_BF16), jnp.where(first, zero, qt).astype(_BF16))

    def straight_line(i):
        state = {}

        def begin():
            qts = load_qt(i)
            prev = jnp.maximum(i - 1, 0)
            state["chains"] = [(qts[0], i, True), (qts[0], prev, False),
                               (qts[1], i, True), (qts[1], prev, False)]
            state["rows0"], state["tots"] = {}, {}

        def softplus_step(c):
            def step():
                state["rows0"][c] = stage_softplus(c, *state["chains"][c])
            return step

        def weights_step(c):
            def step():
                state["tots"][c] = stage_weights(c, state["rows0"][c])
            return step

        def combine_step(hh):
            def step():
                d = 2 * hh
                tots, chains = state["tots"], state["chains"]
                pv_diag, pv_prev = stage_pv(d, chains[d][1]), stage_pv(d + 1, chains[d + 1][1])
                has_prev = jnp.where(i > 0, 1.0, 0.0).astype(_F32)
                acc_ref[hh] = pv_diag + (jnp.exp2(-tots[d]) * has_prev) * pv_prev
                mass_ref[hh] = tots[d] + has_prev * tots[d + 1]
            return step

        return dict(begin=begin, softplus=[softplus_step(c) for c in range(4)],
                    weights=[weights_step(c) for c in range(4)],
                    combine=[combine_step(hh) for hh in range(2)])

    def finish(i):
        def min_mass():
            return jnp.min(jnp.minimum(mass_ref[0], mass_ref[1]))

        def cond(c):
            j, mm = c
            return jnp.logical_and(j >= 0, mm < SB_DEAD_MASS)

        def body(c):
            j, _ = c
            qts = load_qt(i)
            for hh in range(2):
                tot = stage_weights(hh, stage_softplus(hh, qts[hh], j, False))
                mass = mass_ref[hh]
                acc_ref[hh] += jnp.exp2(-mass) * stage_pv(hh, j)
                mass_ref[hh] = mass + tot
            return j - 1, min_mass()

        lax.while_loop(cond, body, (i - 2, min_mass()))
        out_t = jnp.where(first, acc_ref[0], acc_ref[1])
        o_ref[0, pl.ds(pl.multiple_of(i * t, t), t), :] = out_t.T

    return straight_line, finish


def _mla_parts(qt_ref, k_ref, vt_ref, bias_ref, o_ref, acc_ref, m_ref):
    t = MLA_TILE
    n_q = k_ref.shape[1] // t

    def scores_into(dst_ref, i, j):
        i = jnp.minimum(i, n_q - 1)
        j = jnp.minimum(j, n_q - 1)
        qs = pl.multiple_of(i * t, t)
        ks = pl.multiple_of(j * t, t)
        dst_ref[...] = _dot(k_ref[0, pl.ds(ks, t), :], qt_ref[0, 0, :, pl.ds(qs, t)])

    def soften(src_ref, i, diagonal):
        s = src_ref[...]
        if diagonal:
            s = s + bias_ref[...]
        m_old = m_ref[i]
        m_new = jnp.maximum(m_old, jnp.max(s, axis=0, keepdims=True))
        p = jnp.exp2(s - m_new).astype(_BF16)
        if not diagonal:
            m_ref[i] = m_new
        return p, jnp.exp2(m_old - m_new)

    def accumulate(p, alpha, i, j, diagonal):
        ks = pl.multiple_of(j * t, t)
        vt = vt_ref[0, 0, :, pl.ds(ks, t)]
        acc = alpha * acc_ref[i] + _dot(vt, p)
        if diagonal:
            qs = pl.multiple_of(i * t, t)
            o_ref[0, pl.ds(qs, t), :] = (acc[:V_DIM, :] / acc[V_DIM:V_DIM + 1, :]).T
        else:
            acc_ref[i] = acc

    return scores_into, soften, accumulate


def _attn_kernel(qt_ref, km_ref, vtm_ref, bias_ref, qs_ref, ks_ref, vts_ref, tri_ref,
                 om_ref, os_ref, acc_ref, m_ref, sa_ref, sb_ref, sacc_ref, smass_ref, nls_ref,
                 spb_ref, ab_ref):
    scores_into, soften, accumulate = _mla_parts(qt_ref, km_ref, vtm_ref, bias_ref, om_ref, acc_ref, m_ref)
    sb_straight, sb_finish = _sb_parts(qs_ref, ks_ref, vts_ref, tri_ref, os_ref, sacc_ref,
                                       smass_ref, nls_ref, spb_ref, ab_ref)
    n_q = km_ref.shape[1] // MLA_TILE
    n_off = n_q * (n_q - 1) // 2
    n_sb = qs_ref.shape[1] // SB_TILE

    acc_ref[...] = jnp.zeros_like(acc_ref)
    m_ref[...] = jnp.full_like(m_ref, jnp.finfo(_F32).min)

    def advance(i, j):
        wrap = j + 1 >= i
        return jnp.where(wrap, i + 1, i), jnp.where(wrap, 0, j + 1)

    def tile_steps(src_ref, i, j, diagonal):
        state = {}

        def soften_step():
            state["p"], state["alpha"] = soften(src_ref, i, diagonal)

        def accumulate_step():
            accumulate(state["p"], state["alpha"], i, j, diagonal)

        return [soften_step, accumulate_step]

    def update(src_ref, i, j, diagonal):
        for step in tile_steps(src_ref, i, j, diagonal):
            step()

    def off_steps(n, carry):
        i, j = carry
        i1, j1 = advance(i, j)
        i2, j2 = advance(i1, j1)
        return ([lambda: scores_into(sb_ref, i1, j1)] + tile_steps(sa_ref, i, j, False)
                + [lambda: scores_into(sa_ref, i2, j2)]
                + tile_steps(sb_ref, i1, j1, False)), (i2, j2)

    def diag_steps(n, carry):
        i = 2 * n
        return ([lambda: scores_into(sb_ref, i + 1, i + 1)] + tile_steps(sa_ref, i, i, True)
                + [lambda: scores_into(sa_ref, i + 2, i + 2)]
                + tile_steps(sb_ref, i + 1, i + 1, True)), carry

    def plain(mla_steps):
        def body(n, carry):
            steps, carry = mla_steps(n, carry)
            for step in steps:
                step()
            return carry
        return body

    def with_sb(first_sb_tile, mla_steps):
        def body(n, carry):
            mla, carry = mla_steps(n, carry)
            sb = sb_straight(first_sb_tile + n)
            sp, wt, cb = sb["softplus"], sb["weights"], sb["combine"]
            order = [sb["begin"], mla[0], sp[0], mla[1], mla[2], sp[1], wt[0], sp[2], wt[1], mla[3],
                     sp[3], cb[0], wt[2], mla[4], mla[5], wt[3], cb[1]]
            for step in order:
                step()
            sb_finish(first_sb_tile + n)
            return carry
        return body

    sb_done = 0
    trips = n_off // 2
    if n_off:
        scores_into(sa_ref, 1, 0)
        fused = min(trips, n_sb - sb_done)
        carry = (jnp.int32(1), jnp.int32(0))
        carry = lax.fori_loop(0, fused, with_sb(sb_done, off_steps), carry)
        carry = lax.fori_loop(fused, trips, plain(off_steps), carry)
        sb_done += fused
        if n_off % 2:
            update(sa_ref, carry[0], carry[1], False)

    trips = n_q // 2
    scores_into(sa_ref, 0, 0)
    fused = min(trips, n_sb - sb_done)
    lax.fori_loop(0, fused, with_sb(sb_done, diag_steps), 0)
    lax.fori_loop(fused, trips, plain(diag_steps), 0)
    sb_done += fused
    if n_q % 2:
        update(sa_ref, n_q - 1, n_q - 1, True)

    def sb_only(n, carry):
        sb = sb_straight(n)
        for step in [sb["begin"]] + sb["softplus"] + sb["weights"] + sb["combine"]:
            step()
        sb_finish(n)
        return carry

    lax.fori_loop(sb_done, n_sb, sb_only, 0)


def _attention(qtcat, kcat, vmt, bias, qsb, ksb, vtsb, tri):
    bsz, seq, _ = kcat.shape
    assert MLA_HEADS == SB_WIDTH // LANES
    t, ts = MLA_TILE, SB_TILE
    head_rows = lambda b, h: (b, h, 0, 0)
    lane_block = lambda b, h: (b, 0, h)
    return pl.pallas_call(
        _attn_kernel,
        grid=(bsz, MLA_HEADS),
        in_specs=[
            pl.BlockSpec((1, 1, MLA_QK_PAD, seq), head_rows),
            pl.BlockSpec((1, seq, MLA_QK_PAD), lane_block),
            pl.BlockSpec((1, 1, V_EXT, seq), head_rows),
            pl.BlockSpec((t, t), lambda b, h: (0, 0)),
            pl.BlockSpec((1, seq, LANES), lane_block),
            pl.BlockSpec((1, seq, LANES), lane_block),
            pl.BlockSpec((1, LANES, seq), lambda b, h: (b, h, 0)),
            pl.BlockSpec((ts, ts), lambda b, h: (0, 0)),
        ],
        out_specs=(pl.BlockSpec((1, seq, V_DIM), lane_block),
                   pl.BlockSpec((1, seq, LANES), lane_block)),
        out_shape=(jax.ShapeDtypeStruct((bsz, seq, MLA_WIDTH), _F32),
                   jax.ShapeDtypeStruct((bsz, seq, SB_WIDTH), _F32)),
        scratch_shapes=[pltpu.VMEM((seq // t, V_EXT, t), _F32), pltpu.VMEM((seq // t, 1, t), _F32),
                        pltpu.VMEM((t, t), _F32), pltpu.VMEM((t, t), _F32),
                        pltpu.VMEM((2, LANES, ts), _F32), pltpu.VMEM((2, 1, ts), _F32),
                        pltpu.VMEM((4, ts, ts), _BF16), pltpu.VMEM((4, ts, ts), _BF16),
                        pltpu.VMEM((4, ts, ts), _BF16)],
        compiler_params=pltpu.CompilerParams(
            dimension_semantics=("arbitrary", "arbitrary"), vmem_limit_bytes=VMEM_LIMIT),
        name="attention",
    )(qtcat, kcat, vmt, bias, qsb, ksb, vtsb, tri)


def _post_kernel(x_ref, osb_ref, omla_ref, g_sb_ref, g_mla_ref, wo_ref, g_mlp_ref, wup_ref,
                 wdown_ref, g_out_ref, out_ref, *, apply_out_norm):
    ms = _rms(osb_ref[...], g_sb_ref[...]).astype(_BF16)
    mm = _rms(omla_ref[...], g_mla_ref[...]).astype(_BF16)
    h1 = x_ref[...] + _dot(jnp.concatenate([ms, mm], axis=1), wo_ref[...])
    v = _rms(h1, g_mlp_ref[...]).astype(_BF16)
    hid = jnp.square(jnp.maximum(_dot(v, wup_ref[...]), 0.0)).astype(_BF16)
    h2 = h1 + _dot(hid, wdown_ref[...])
    out_ref[...] = _rms(h2, g_out_ref[...]) if apply_out_norm else h2


def _post(x2, osb2, omla2, g_sb, g_mla, wo, g_mlp, wup, wdown, g_out, apply_out_norm):
    n, d = x2.shape
    tm = POST_TILE
    once = pl.Buffered(1)
    rows = lambda w: pl.BlockSpec((tm, w), lambda r: (r, 0))
    const = lambda a: pl.BlockSpec(a.shape, lambda r: (0, 0), pipeline_mode=once)
    return pl.pallas_call(
        functools.partial(_post_kernel, apply_out_norm=apply_out_norm),
        grid=(n // tm,),
        in_specs=[rows(d), rows(SB_WIDTH), rows(MLA_WIDTH), const(g_sb), const(g_mla), const(wo),
                  const(g_mlp), const(wup), const(wdown), const(g_out)],
        out_specs=rows(d),
        out_shape=jax.ShapeDtypeStruct((n, d), _F32),
        compiler_params=pltpu.CompilerParams(
            dimension_semantics=("arbitrary",), vmem_limit_bytes=VMEM_LIMIT),
        name="post",
    )(x2, osb2, omla2, g_sb, g_mla, wo, g_mlp, wup, wdown, g_out)


def _swap_halves(w):
    half = QK_ROPE // 2
    return jnp.concatenate([w[..., half:], w[..., :half]], axis=-1)


def _layer_weights(w_in, w_q_b, w_kv_b):
    d = w_in.shape[0]
    s2, s3 = 2 * SB_WIDTH, 3 * SB_WIDTH
    s5 = s3 + Q_LORA + KV_LORA
    w1 = jnp.concatenate([w_in[:, :s2], w_in[:, s3:s5]], axis=1).astype(_BF16)
    wvt = w_in[:, s2:s3].T.astype(_BF16)
    wkr = w_in[:, s5:]
    pad = jnp.zeros((d, LANES - QK_ROPE), w_in.dtype)
    wkrt = jnp.concatenate([wkr, pad, _swap_halves(wkr), pad], axis=1).T.astype(_BF16)
    wqb = w_q_b.reshape(Q_LORA, MLA_HEADS, QK_NOPE + QK_ROPE)
    qn = wqb[:, :, :QK_NOPE].reshape(Q_LORA, MLA_HEADS * QK_NOPE)
    qr = wqb[:, :, QK_NOPE:]
    qpad = jnp.zeros((Q_LORA, MLA_HEADS, LANES - QK_ROPE), w_q_b.dtype)
    qr_pad = jnp.concatenate([qr, qpad], axis=-1).reshape(Q_LORA, MLA_HEADS * LANES)
    qrr_pad = jnp.concatenate([_swap_halves(qr), qpad], axis=-1).reshape(Q_LORA, MLA_HEADS * LANES)
    w2t = jnp.concatenate([qn, qr_pad, qrr_pad], axis=1).T.astype(_BF16)
    wkvb = w_kv_b.reshape(KV_LORA, MLA_HEADS, QK_NOPE + V_DIM)
    wkn = wkvb[:, :, :QK_NOPE].reshape(KV_LORA, MLA_HEADS * QK_NOPE).astype(_BF16)
    wvmt = wkvb[:, :, QK_NOPE:].reshape(KV_LORA, MLA_HEADS * V_DIM).T.astype(_BF16)
    return w1, wvt, wkrt, w2t, wkn, wvmt


def kernel(x, positions, attn_norm_g, w_in, q_a_norm_g, w_q_b, kv_a_norm_g, w_kv_b, sb_out_norm_g,
           mla_out_norm_g, w_o, mlp_norm_g, w_up, w_down, final_norm_g):
    bsz, seq, d = x.shape
    depth = w_in.shape[0]
    assert seq % PROJ_TILE == 0 and seq % SB_TILE == 0 and seq % MLA_TILE == 0
    assert (bsz * seq) % POST_TILE == 0

    half = QK_ROPE // 2
    inv_freq = ROPE_BASE ** (-jnp.arange(half, dtype=_F32) / half)
    invf = inv_freq[:, None]
    pos_row = positions[:, None, :]
    r = lax.broadcasted_iota(jnp.int32, (SB_TILE, SB_TILE), 0)
    c = lax.broadcasted_iota(jnp.int32, (SB_TILE, SB_TILE), 1)
    tri = (c > r).astype(_BF16)
    key = lax.broadcasted_iota(jnp.int32, (MLA_TILE, MLA_TILE), 0)
    qry = lax.broadcasted_iota(jnp.int32, (MLA_TILE, MLA_TILE), 1)
    bias = jnp.where(key <= qry, 0.0, MLA_MASK_BIAS).astype(_F32)

    row = lambda g: g[None, :]
    h = x
    for l in range(depth):
        w1, wvt, wkrt, w2t, wkn, wvmt = _layer_weights(w_in[l], w_q_b[l], w_kv_b[l])
        qsb, ksb, vtsb, qtcat, kcat, vmt = _proj(
            h, pos_row, invf, row(attn_norm_g[l]), w1, wvt, wkrt, row(q_a_norm_g[l]),
            row(kv_a_norm_g[l]), w2t, wkn, wvmt)
        omla, osb = _attention(qtcat, kcat, vmt, bias, qsb, ksb, vtsb, tri)
        last = l == depth - 1
        h = _post(h.reshape(bsz * seq, d), osb.reshape(bsz * seq, SB_WIDTH),
                  omla.reshape(bsz * seq, MLA_WIDTH), row(sb_out_norm_g[l]),
                  row(mla_out_norm_g[l]), w_o[l].astype(_BF16), row(mlp_norm_g[l]),
                  w_up[l].astype(_BF16), w_down[l].astype(_BF16), row(final_norm_g),
                  last).reshape(bsz, seq, d)
    return h
```

```python
import functools

import jax
import jax.numpy as jnp
from jax import lax
from jax.experimental import pallas as pl
from jax.experimental.pallas import tpu as pltpu

SB_HEADS = 8
SB_HEAD_DIM = 64
SB_WIDTH = SB_HEADS * SB_HEAD_DIM
SB_SCALE = SB_HEAD_DIM ** -0.5
MLA_HEADS = 4
QK_NOPE = 128
QK_ROPE = 64
V_DIM = 128
Q_LORA = 256
KV_LORA = 128
MLA_WIDTH = MLA_HEADS * V_DIM
MLA_SCALE = (QK_NOPE + QK_ROPE) ** -0.5
LOG2_E = 1.4426950408889634
ROPE_BASE = 10000.0
NORM_EPS = 1e-6

LANES = 128
MLA_QK_PAD = 256
V_EXT = V_DIM + 16
PROJ_TILE = 512
SB_TILE = 256
MLA_TILE = 512
POST_TILE = 512
VMEM_LIMIT = 56 * 1024 * 1024
SB_DEAD_MASS = 158.0
SB_MASKED_SCORE = -1e30
MLA_MASK_BIAS = -1e30

_F32 = jnp.float32
_BF16 = jnp.bfloat16


def _dot(a, b):
    return jnp.dot(a, b, preferred_element_type=_F32)


def _dot_nt(a, b):
    return lax.dot_general(a, b, (((1,), (1,)), ((), ())), preferred_element_type=_F32)


def _rms(x, g):
    y = x * lax.rsqrt(jnp.mean(x * x, axis=-1, keepdims=True) + NORM_EPS)
    return y * g


def _proj_kernel(x_ref, pos_ref, invf_ref, g_attn_ref, w1_ref, wvt_ref, wkrt_ref,
                 g_qa_ref, g_kva_ref, w2t_ref, wkn_ref, wvmt_ref,
                 qsb_ref, ksb_ref, vtsb_ref, qtcat_ref, kcat_ref, vmt_ref):
    u = _rms(x_ref[0], g_attn_ref[...]).astype(_BF16)
    p1 = _dot(u, w1_ref[...])
    qsb_ref[0] = (p1[:, :SB_WIDTH] * (SB_SCALE * LOG2_E)).astype(_BF16)
    ksb_ref[0] = p1[:, SB_WIDTH:2 * SB_WIDTH].astype(_BF16)
    vtsb_ref[0] = _dot_nt(wvt_ref[...], u).astype(_BF16)

    o = 2 * SB_WIDTH
    cqn = _rms(p1[:, o:o + Q_LORA], g_qa_ref[...]).astype(_BF16)
    o += Q_LORA
    ckvn = _rms(p1[:, o:o + KV_LORA], g_kva_ref[...]).astype(_BF16)

    ang = invf_ref[...] * pos_ref[0].astype(_F32)
    cos_h = jnp.cos(ang)
    sin_h = jnp.sin(ang)
    reps = LANES // QK_ROPE
    cost = jnp.concatenate([cos_h, cos_h] * reps, axis=0)
    sint = jnp.concatenate([-sin_h, sin_h] * reps, axis=0)

    krt = _dot_nt(wkrt_ref[...], u)
    kr_roped = (krt[:LANES] * cost + krt[LANES:] * sint).T.astype(_BF16)

    p2t = _dot_nt(w2t_ref[...], cqn)
    kn = _dot(ckvn, wkn_ref[...]).astype(_BF16)
    hw = MLA_HEADS * LANES
    for h in range(MLA_HEADS):
        sl = slice(h * LANES, (h + 1) * LANES)
        qn = p2t[sl] * (MLA_SCALE * LOG2_E)
        qr = (p2t[hw + h * LANES:hw + (h + 1) * LANES] * cost
              + p2t[2 * hw + h * LANES:2 * hw + (h + 1) * LANES] * sint) * (MLA_SCALE * LOG2_E)
        qtcat_ref[0, h, :LANES, :] = qn.astype(_BF16)
        qtcat_ref[0, h, LANES:, :] = qr.astype(_BF16)
        kcat_ref[0, :, h * MLA_QK_PAD:h * MLA_QK_PAD + LANES] = kn[:, sl]
        kcat_ref[0, :, h * MLA_QK_PAD + LANES:(h + 1) * MLA_QK_PAD] = kr_roped
    vt = _dot_nt(wvmt_ref[...], ckvn).astype(_BF16)
    ts = vt.shape[1]
    ones_row = (lax.broadcasted_iota(jnp.int32, (V_EXT - V_DIM, ts), 0) == 0).astype(_BF16)
    for h in range(MLA_HEADS):
        vmt_ref[0, h, :V_DIM, :] = vt[h * V_DIM:(h + 1) * V_DIM]
        vmt_ref[0, h, V_DIM:, :] = ones_row


def _full(shape):
    return pl.BlockSpec(shape, lambda *_: (0,) * len(shape))


def _proj(x, pos_row, invf, g_attn, w1, wvt, wkrt, g_qa, g_kva, w2t, wkn, wvmt):
    bsz, seq, d = x.shape
    ts = PROJ_TILE
    grid = (bsz, seq // ts)
    row = lambda b, s: (b, s, 0)
    col = lambda b, s: (b, 0, s)
    out_shape = (
        jax.ShapeDtypeStruct((bsz, seq, SB_WIDTH), _BF16),
        jax.ShapeDtypeStruct((bsz, seq, SB_WIDTH), _BF16),
        jax.ShapeDtypeStruct((bsz, SB_WIDTH, seq), _BF16),
        jax.ShapeDtypeStruct((bsz, MLA_HEADS, MLA_QK_PAD, seq), _BF16),
        jax.ShapeDtypeStruct((bsz, seq, MLA_HEADS * MLA_QK_PAD), _BF16),
        jax.ShapeDtypeStruct((bsz, MLA_HEADS, V_EXT, seq), _BF16),
    )
    heads_col = lambda b, s: (b, 0, 0, s)
    return pl.pallas_call(
        _proj_kernel,
        grid=grid,
        in_specs=[
            pl.BlockSpec((1, ts, d), row),
            pl.BlockSpec((1, 1, ts), col),
            _full(invf.shape), _full(g_attn.shape), _full(w1.shape),
            _full(wvt.shape), _full(wkrt.shape), _full(g_qa.shape), _full(g_kva.shape),
            _full(w2t.shape), _full(wkn.shape), _full(wvmt.shape),
        ],
        out_specs=(
            pl.BlockSpec((1, ts, SB_WIDTH), row),
            pl.BlockSpec((1, ts, SB_WIDTH), row),
            pl.BlockSpec((1, SB_WIDTH, ts), col),
            pl.BlockSpec((1, MLA_HEADS, MLA_QK_PAD, ts), heads_col),
            pl.BlockSpec((1, ts, MLA_HEADS * MLA_QK_PAD), row),
            pl.BlockSpec((1, MLA_HEADS, V_EXT, ts), heads_col),
        ),
        out_shape=out_shape,
        compiler_params=pltpu.CompilerParams(
            dimension_semantics=("arbitrary", "arbitrary"), vmem_limit_bytes=VMEM_LIMIT),
        name="proj",
    )(x, pos_row, invf, g_attn, w1, wvt, wkrt, g_qa, g_kva, w2t, wkn, wvmt)


def _sb_parts(q_ref, k_ref, vt_ref, tri_ref, o_ref, acc_ref, mass_ref, minmass_ref, nls_ref,
              spb_ref, ab_ref):
    t = SB_TILE
    tri = tri_ref[...]
    sub = lax.broadcasted_iota(jnp.int32, (LANES, t), 0)
    first = sub < SB_HEAD_DIM

    def strict_mask():
        key = lax.broadcasted_iota(jnp.int32, (t, t), 0)
        qry = lax.broadcasted_iota(jnp.int32, (t, t), 1)
        return key < qry

    def stage_softplus(c, qt, j, masked):
        start = pl.multiple_of(j * t, t)
        z = _dot(k_ref[0, pl.ds(start, t), :], qt)
        if masked:
            z = jnp.where(strict_mask(), z, SB_MASKED_SCORE)
        zb = z.astype(_BF16)
        pos = jnp.maximum(zb, 0.0)
        neg = zb - pos
        lg = jnp.log2((1.0 + jnp.exp2(neg - pos)).astype(_F32)).astype(_BF16)
        sp = pos + lg
        spb_ref[c] = sp
        nls_ref[c] = neg - lg
        return sp[:1, :].astype(_F32)

    def stage_weights(c, sp_row0):
        excl = _dot(tri, spb_ref[c])
        ab_ref[c] = jnp.exp2(nls_ref[c] - excl.astype(_BF16))
        return excl[:1, :] + sp_row0

    def stage_pv(c, j):
        start = pl.multiple_of(j * t, t)
        return _dot(vt_ref[0, :, pl.ds(start, t)], ab_ref[c])

    def load_qt(i):
        start = pl.multiple_of(i * t, t)
        qt = q_ref[0, pl.ds(start, t), :].astype(_F32).T
        zero = jnp.zeros_like(qt)
        return (jnp.where(first, qt, zero).astype(_BF16), jnp.where(first, zero, qt).astype(_BF16))

    def straight_line(i):
        state = {}

        def begin():
            qts = load_qt(i)
            prev = jnp.maximum(i - 1, 0)
            state["chains"] = [(qts[0], i, True), (qts[0], prev, False),
                               (qts[1], i, True), (qts[1], prev, False)]
            state["rows0"], state["tots"] = {}, {}

        def softplus_step(c):
            def step():
                state["rows0"][c] = stage_softplus(c, *state["chains"][c])
            return step

        def weights_step(c):
            def step():
                state["tots"][c] = stage_weights(c, state["rows0"][c])
            return step

        def combine_step(hh):
            def step():
                d = 2 * hh
                tots, chains = state["tots"], state["chains"]
                pv_diag, pv_prev = stage_pv(d, chains[d][1]), stage_pv(d + 1, chains[d + 1][1])
                has_prev = jnp.where(i > 0, 1.0, 0.0).astype(_F32)
                acc = pv_diag + (jnp.exp2(-tots[d]) * has_prev) * pv_prev
                mass = tots[d] + has_prev * tots[d + 1]
                acc_ref[i, hh] = acc
                mass_ref[i, hh] = mass
                state["acc", hh] = acc
                minmass_ref[...] = jnp.minimum(minmass_ref[...], mass)
                if hh == 1:
                    out_t = jnp.where(first, state["acc", 0], acc)
                    o_ref[0, pl.ds(pl.multiple_of(i * t, t), t), :] = out_t.T
            return step

        return dict(begin=begin, softplus=[softplus_step(c) for c in range(4)],
                    weights=[weights_step(c) for c in range(4)],
                    combine=[combine_step(hh) for hh in range(2)])

    def sweep_rest(i):
        def min_mass():
            return jnp.min(jnp.minimum(mass_ref[i, 0], mass_ref[i, 1]))

        def cond(c):
            j, mm = c
            return jnp.logical_and(j >= 0, mm < SB_DEAD_MASS)

        def body(c):
            j, _ = c
            qts = load_qt(i)
            for hh in range(2):
                tot = stage_weights(hh, stage_softplus(hh, qts[hh], j, False))
                mass = mass_ref[i, hh]
                acc_ref[i, hh] += jnp.exp2(-mass) * stage_pv(hh, j)
                mass_ref[i, hh] = mass + tot
            return j - 1, min_mass()

        lax.while_loop(cond, body, (i - 2, min_mass()))
        out_t = jnp.where(first, acc_ref[i, 0], acc_ref[i, 1])
        o_ref[0, pl.ds(pl.multiple_of(i * t, t), t), :] = out_t.T

    def finish_all(n_tiles):
        @pl.when(jnp.min(minmass_ref[...]) < SB_DEAD_MASS)
        def _():
            def one(i, carry):
                sweep_rest(i)
                return carry
            lax.fori_loop(0, n_tiles, one, 0)

    return straight_line, finish_all


def _mla_parts(qt_ref, k_ref, vt_ref, bias_ref, o_ref, acc_ref, m_ref):
    t = MLA_TILE
    n_q = k_ref.shape[1] // t

    def scores_into(dst_ref, i, j):
        i = jnp.minimum(i, n_q - 1)
        j = jnp.minimum(j, n_q - 1)
        qs = pl.multiple_of(i * t, t)
        ks = pl.multiple_of(j * t, t)
        dst_ref[...] = _dot(k_ref[0, pl.ds(ks, t), :], qt_ref[0, 0, :, pl.ds(qs, t)])

    def soften(src_ref, i, diagonal):
        s = src_ref[...]
        if diagonal:
            s = s + bias_ref[...]
        m_old = m_ref[i]
        m_new = jnp.maximum(m_old, jnp.max(s, axis=0, keepdims=True))
        p = jnp.exp2(s - m_new).astype(_BF16)
        if not diagonal:
            m_ref[i] = m_new
        return p, jnp.exp2(m_old - m_new)

    def accumulate(p, alpha, i, j, diagonal):
        ks = pl.multiple_of(j * t, t)
        vt = vt_ref[0, 0, :, pl.ds(ks, t)]
        acc = alpha * acc_ref[i] + _dot(vt, p)
        if diagonal:
            qs = pl.multiple_of(i * t, t)
            o_ref[0, pl.ds(qs, t), :] = (acc[:V_DIM, :] / acc[V_DIM:V_DIM + 1, :]).T
        else:
            acc_ref[i] = acc

    return scores_into, soften, accumulate


def _attn_kernel(qt_ref, km_ref, vtm_ref, bias_ref, qs_ref, ks_ref, vts_ref, tri_ref,
                 om_ref, os_ref, acc_ref, m_ref, sa_ref, sb_ref, sacc_ref, smass_ref, minmass_ref,
                 nls_ref, spb_ref, ab_ref):
    scores_into, soften, accumulate = _mla_parts(qt_ref, km_ref, vtm_ref, bias_ref, om_ref, acc_ref, m_ref)
    sb_straight, sb_finish_all = _sb_parts(qs_ref, ks_ref, vts_ref, tri_ref, os_ref, sacc_ref,
                                           smass_ref, minmass_ref, nls_ref, spb_ref, ab_ref)
    n_q = km_ref.shape[1] // MLA_TILE
    n_off = n_q * (n_q - 1) // 2
    n_sb = qs_ref.shape[1] // SB_TILE

    acc_ref[...] = jnp.zeros_like(acc_ref)
    m_ref[...] = jnp.full_like(m_ref, jnp.finfo(_F32).min)
    minmass_ref[...] = jnp.full_like(minmass_ref, jnp.finfo(_F32).max)

    def advance(i, j):
        wrap = j + 1 >= i
        return jnp.where(wrap, i + 1, i), jnp.where(wrap, 0, j + 1)

    def tile_steps(src_ref, i, j, diagonal):
        state = {}

        def soften_step():
            state["p"], state["alpha"] = soften(src_ref, i, diagonal)

        def accumulate_step():
            accumulate(state["p"], state["alpha"], i, j, diagonal)

        return [soften_step, accumulate_step]

    def update(src_ref, i, j, diagonal):
        for step in tile_steps(src_ref, i, j, diagonal):
            step()

    def off_steps(n, carry):
        i, j = carry
        i1, j1 = advance(i, j)
        i2, j2 = advance(i1, j1)
        return ([lambda: scores_into(sb_ref, i1, j1)] + tile_steps(sa_ref, i, j, False)
                + [lambda: scores_into(sa_ref, i2, j2)]
                + tile_steps(sb_ref, i1, j1, False)), (i2, j2)

    def diag_steps(n, carry):
        i = 2 * n
        return ([lambda: scores_into(sb_ref, i + 1, i + 1)] + tile_steps(sa_ref, i, i, True)
                + [lambda: scores_into(sa_ref, i + 2, i + 2)]
                + tile_steps(sb_ref, i + 1, i + 1, True)), carry

    def plain(mla_steps):
        def body(n, carry):
            steps, carry = mla_steps(n, carry)
            for step in steps:
                step()
            return carry
        return body

    def with_sb(first_sb_tile, mla_steps):
        def body(n, carry):
            mla, carry = mla_steps(n, carry)
            sb = sb_straight(first_sb_tile + n)
            sp, wt, cb = sb["softplus"], sb["weights"], sb["combine"]
            order = [sb["begin"], mla[0], sp[0], mla[1], mla[2], sp[1], wt[0], sp[2], wt[1], mla[3],
                     sp[3], cb[0], wt[2], mla[4], mla[5], wt[3], cb[1]]
            for step in order:
                step()
            return carry
        return body

    sb_done = 0
    trips = n_off // 2
    if n_off:
        scores_into(sa_ref, 1, 0)
        fused = min(trips, n_sb - sb_done)
        carry = (jnp.int32(1), jnp.int32(0))
        carry = lax.fori_loop(0, fused, with_sb(sb_done, off_steps), carry)
        carry = lax.fori_loop(fused, trips, plain(off_steps), carry)
        sb_done += fused
        if n_off % 2:
            update(sa_ref, carry[0], carry[1], False)

    trips = n_q // 2
    scores_into(sa_ref, 0, 0)
    fused = min(trips, n_sb - sb_done)
    lax.fori_loop(0, fused, with_sb(sb_done, diag_steps), 0)
    lax.fori_loop(fused, trips, plain(diag_steps), 0)
    sb_done += fused
    if n_q % 2:
        update(sa_ref, n_q - 1, n_q - 1, True)

    def sb_only(n, carry):
        sb = sb_straight(n)
        for step in [sb["begin"]] + sb["softplus"] + sb["weights"] + sb["combine"]:
            step()
        return carry

    lax.fori_loop(sb_done, n_sb, sb_only, 0)
    sb_finish_all(n_sb)


def _attention(qtcat, kcat, vmt, bias, qsb, ksb, vtsb, tri):
    bsz, seq, _ = kcat.shape
    assert MLA_HEADS == SB_WIDTH // LANES
    t, ts = MLA_TILE, SB_TILE
    head_rows = lambda b, h: (b, h, 0, 0)
    lane_block = lambda b, h: (b, 0, h)
    return pl.pallas_call(
        _attn_kernel,
        grid=(bsz, MLA_HEADS),
        in_specs=[
            pl.BlockSpec((1, 1, MLA_QK_PAD, seq), head_rows),
            pl.BlockSpec((1, seq, MLA_QK_PAD), lane_block),
            pl.BlockSpec((1, 1, V_EXT, seq), head_rows),
            pl.BlockSpec((t, t), lambda b, h: (0, 0)),
            pl.BlockSpec((1, seq, LANES), lane_block),
            pl.BlockSpec((1, seq, LANES), lane_block),
            pl.BlockSpec((1, LANES, seq), lambda b, h: (b, h, 0)),
            pl.BlockSpec((ts, ts), lambda b, h: (0, 0)),
        ],
        out_specs=(pl.BlockSpec((1, seq, V_DIM), lane_block),
                   pl.BlockSpec((1, seq, LANES), lane_block)),
        out_shape=(jax.ShapeDtypeStruct((bsz, seq, MLA_WIDTH), _F32),
                   jax.ShapeDtypeStruct((bsz, seq, SB_WIDTH), _F32)),
        scratch_shapes=[pltpu.VMEM((seq // t, V_EXT, t), _F32), pltpu.VMEM((seq // t, 1, t), _F32),
                        pltpu.VMEM((t, t), _F32), pltpu.VMEM((t, t), _F32),
                        pltpu.VMEM((seq // ts, 2, LANES, ts), _F32),
                        pltpu.VMEM((seq // ts, 2, 1, ts), _F32), pltpu.VMEM((1, ts), _F32),
                        pltpu.VMEM((4, ts, ts), _BF16), pltpu.VMEM((4, ts, ts), _BF16),
                        pltpu.VMEM((4, ts, ts), _BF16)],
        compiler_params=pltpu.CompilerParams(
            dimension_semantics=("arbitrary", "arbitrary"), vmem_limit_bytes=VMEM_LIMIT),
        name="attention",
    )(qtcat, kcat, vmt, bias, qsb, ksb, vtsb, tri)


def _post_kernel(x_ref, osb_ref, omla_ref, g_sb_ref, g_mla_ref, wo_ref, g_mlp_ref, wup_ref,
                 wdown_ref, g_out_ref, out_ref, *, apply_out_norm):
    ms = _rms(osb_ref[...], g_sb_ref[...]).astype(_BF16)
    mm = _rms(omla_ref[...], g_mla_ref[...]).astype(_BF16)
    h1 = x_ref[...] + _dot(jnp.concatenate([ms, mm], axis=1), wo_ref[...])
    v = _rms(h1, g_mlp_ref[...]).astype(_BF16)
    hid = jnp.square(jnp.maximum(_dot(v, wup_ref[...]), 0.0)).astype(_BF16)
    h2 = h1 + _dot(hid, wdown_ref[...])
    out_ref[...] = _rms(h2, g_out_ref[...]) if apply_out_norm else h2


def _post(x2, osb2, omla2, g_sb, g_mla, wo, g_mlp, wup, wdown, g_out, apply_out_norm):
    n, d = x2.shape
    tm = POST_TILE
    once = pl.Buffered(1)
    rows = lambda w: pl.BlockSpec((tm, w), lambda r: (r, 0))
    const = lambda a: pl.BlockSpec(a.shape, lambda r: (0, 0), pipeline_mode=once)
    return pl.pallas_call(
        functools.partial(_post_kernel, apply_out_norm=apply_out_norm),
        grid=(n // tm,),
        in_specs=[rows(d), rows(SB_WIDTH), rows(MLA_WIDTH), const(g_sb), const(g_mla), const(wo),
                  const(g_mlp), const(wup), const(wdown), const(g_out)],
        out_specs=rows(d),
        out_shape=jax.ShapeDtypeStruct((n, d), _F32),
        compiler_params=pltpu.CompilerParams(
            dimension_semantics=("arbitrary",), vmem_limit_bytes=VMEM_LIMIT),
        name="post",
    )(x2, osb2, omla2, g_sb, g_mla, wo, g_mlp, wup, wdown, g_out)


def _swap_halves(w):
    half = QK_ROPE // 2
    return jnp.concatenate([w[..., half:], w[..., :half]], axis=-1)


def _layer_weights(w_in, w_q_b, w_kv_b):
    d = w_in.shape[0]
    s2, s3 = 2 * SB_WIDTH, 3 * SB_WIDTH
    s5 = s3 + Q_LORA + KV_LORA
    w1 = jnp.concatenate([w_in[:, :s2], w_in[:, s3:s5]], axis=1).astype(_BF16)
    wvt = w_in[:, s2:s3].T.astype(_BF16)
    wkr = w_in[:, s5:]
    pad = jnp.zeros((d, LANES - QK_ROPE), w_in.dtype)
    wkrt = jnp.concatenate([wkr, pad, _swap_halves(wkr), pad], axis=1).T.astype(_BF16)
    wqb = w_q_b.reshape(Q_LORA, MLA_HEADS, QK_NOPE + QK_ROPE)
    qn = wqb[:, :, :QK_NOPE].reshape(Q_LORA, MLA_HEADS * QK_NOPE)
    qr = wqb[:, :, QK_NOPE:]
    qpad = jnp.zeros((Q_LORA, MLA_HEADS, LANES - QK_ROPE), w_q_b.dtype)
    qr_pad = jnp.concatenate([qr, qpad], axis=-1).reshape(Q_LORA, MLA_HEADS * LANES)
    qrr_pad = jnp.concatenate([_swap_halves(qr), qpad], axis=-1).reshape(Q_LORA, MLA_HEADS * LANES)
    w2t = jnp.concatenate([qn, qr_pad, qrr_pad], axis=1).T.astype(_BF16)
    wkvb = w_kv_b.reshape(KV_LORA, MLA_HEADS, QK_NOPE + V_DIM)
    wkn = wkvb[:, :, :QK_NOPE].reshape(KV_LORA, MLA_HEADS * QK_NOPE).astype(_BF16)
    wvmt = wkvb[:, :, QK_NOPE:].reshape(KV_LORA, MLA_HEADS * V_DIM).T.astype(_BF16)
    return w1, wvt, wkrt, w2t, wkn, wvmt


def kernel(x, positions, attn_norm_g, w_in, q_a_norm_g, w_q_b, kv_a_norm_g, w_kv_b, sb_out_norm_g,
           mla_out_norm_g, w_o, mlp_norm_g, w_up, w_down, final_norm_g):
    bsz, seq, d = x.shape
    depth = w_in.shape[0]
    assert seq % PROJ_TILE == 0 and seq % SB_TILE == 0 and seq % MLA_TILE == 0
    assert (bsz * seq) % POST_TILE == 0

    half = QK_ROPE // 2
    inv_freq = ROPE_BASE ** (-jnp.arange(half, dtype=_F32) / half)
    invf = inv_freq[:, None]
    pos_row = positions[:, None, :]
    r = lax.broadcasted_iota(jnp.int32, (SB_TILE, SB_TILE), 0)
    c = lax.broadcasted_iota(jnp.int32, (SB_TILE, SB_TILE), 1)
    tri = (c > r).astype(_BF16)
    key = lax.broadcasted_iota(jnp.int32, (MLA_TILE, MLA_TILE), 0)
    qry = lax.broadcasted_iota(jnp.int32, (MLA_TILE, MLA_TILE), 1)
    bias = jnp.where(key <= qry, 0.0, MLA_MASK_BIAS).astype(_F32)

    row = lambda g: g[None, :]
    h = x
    for l in range(depth):
        w1, wvt, wkrt, w2t, wkn, wvmt = _layer_weights(w_in[l], w_q_b[l], w_kv_b[l])
        qsb, ksb, vtsb, qtcat, kcat, vmt = _proj(
            h, pos_row, invf, row(attn_norm_g[l]), w1, wvt, wkrt, row(q_a_norm_g[l]),
            row(kv_a_norm_g[l]), w2t, wkn, wvmt)
        omla, osb = _attention(qtcat, kcat, vmt, bias, qsb, ksb, vtsb, tri)
        last = l == depth - 1
        h = _post(h.reshape(bsz * seq, d), osb.reshape(bsz * seq, SB_WIDTH),
                  omla.reshape(bsz * seq, MLA_WIDTH), row(sb_out_norm_g[l]),
                  row(mla_out_norm_g[l]), w_o[l].astype(_BF16), row(mlp_norm_g[l]),
                  w_up[l].astype(_BF16), w_down[l].astype(_BF16), row(final_norm_g),
                  last).reshape(bsz, seq, d)
    return h
```

```python
import functools

import jax
import jax.numpy as jnp
from jax import lax
from jax.experimental import pallas as pl
from jax.experimental.pallas import tpu as pltpu

SB_HEADS = 8
SB_HEAD_DIM = 64
SB_WIDTH = SB_HEADS * SB_HEAD_DIM
SB_SCALE = SB_HEAD_DIM ** -0.5
MLA_HEADS = 4
QK_NOPE = 128
QK_ROPE = 64
V_DIM = 128
Q_LORA = 256
KV_LORA = 128
MLA_WIDTH = MLA_HEADS * V_DIM
MLA_SCALE = (QK_NOPE + QK_ROPE) ** -0.5
LOG2_E = 1.4426950408889634
ROPE_BASE = 10000.0
NORM_EPS = 1e-6

LANES = 128
MLA_QK_PAD = 256
V_EXT = V_DIM + 16
PROJ_TILE = 512
SB_TILE = 256
MLA_TILE = 512
POST_TILE = 512
VMEM_LIMIT = 56 * 1024 * 1024
SB_DEAD_MASS = 158.0
SB_MASKED_SCORE = -1e30
MLA_MASK_BIAS = -1e30

_F32 = jnp.float32
_BF16 = jnp.bfloat16


def _dot(a, b):
    return jnp.dot(a, b, preferred_element_type=_F32)


def _dot_nt(a, b):
    return lax.dot_general(a, b, (((1,), (1,)), ((), ())), preferred_element_type=_F32)


def _rms(x, g):
    y = x * lax.rsqrt(jnp.mean(x * x, axis=-1, keepdims=True) + NORM_EPS)
    return y * g


def _proj_kernel(x_ref, pos_ref, invf_ref, g_attn_ref, w1_ref, wvt_ref, wkrt_ref,
                 g_qa_ref, g_kva_ref, w2t_ref, wkn_ref, wvmt_ref,
                 qsb_ref, ksb_ref, vtsb_ref, qtcat_ref, kcat_ref, vmt_ref):
    u = _rms(x_ref[0], g_attn_ref[...]).astype(_BF16)
    p1 = _dot(u, w1_ref[...])
    qsb_ref[0] = (p1[:, :SB_WIDTH] * (SB_SCALE * LOG2_E)).astype(_BF16)
    ksb_ref[0] = p1[:, SB_WIDTH:2 * SB_WIDTH].astype(_BF16)
    vtsb_ref[0] = _dot_nt(wvt_ref[...], u).astype(_BF16)

    o = 2 * SB_WIDTH
    cqn = _rms(p1[:, o:o + Q_LORA], g_qa_ref[...]).astype(_BF16)
    o += Q_LORA
    ckvn = _rms(p1[:, o:o + KV_LORA], g_kva_ref[...]).astype(_BF16)

    ang = invf_ref[...] * pos_ref[0].astype(_F32)
    cos_h = jnp.cos(ang)
    sin_h = jnp.sin(ang)
    reps = LANES // QK_ROPE
    cost = jnp.concatenate([cos_h, cos_h] * reps, axis=0)
    sint = jnp.concatenate([-sin_h, sin_h] * reps, axis=0)

    krt = _dot_nt(wkrt_ref[...], u)
    kr_roped = (krt[:LANES] * cost + krt[LANES:] * sint).T.astype(_BF16)

    p2t = _dot_nt(w2t_ref[...], cqn)
    kn = _dot(ckvn, wkn_ref[...]).astype(_BF16)
    hw = MLA_HEADS * LANES
    for h in range(MLA_HEADS):
        sl = slice(h * LANES, (h + 1) * LANES)
        qn = p2t[sl] * (MLA_SCALE * LOG2_E)
        qr = (p2t[hw + h * LANES:hw + (h + 1) * LANES] * cost
              + p2t[2 * hw + h * LANES:2 * hw + (h + 1) * LANES] * sint) * (MLA_SCALE * LOG2_E)
        qtcat_ref[0, h, :LANES, :] = qn.astype(_BF16)
        qtcat_ref[0, h, LANES:, :] = qr.astype(_BF16)
        kcat_ref[0, :, h * MLA_QK_PAD:h * MLA_QK_PAD + LANES] = kn[:, sl]
        kcat_ref[0, :, h * MLA_QK_PAD + LANES:(h + 1) * MLA_QK_PAD] = kr_roped
    vt = _dot_nt(wvmt_ref[...], ckvn).astype(_BF16)
    ts = vt.shape[1]
    ones_row = (lax.broadcasted_iota(jnp.int32, (V_EXT - V_DIM, ts), 0) == 0).astype(_BF16)
    for h in range(MLA_HEADS):
        vmt_ref[0, h, :V_DIM, :] = vt[h * V_DIM:(h + 1) * V_DIM]
        vmt_ref[0, h, V_DIM:, :] = ones_row


def _full(shape):
    return pl.BlockSpec(shape, lambda *_: (0,) * len(shape))


def _proj(x, pos_row, invf, g_attn, w1, wvt, wkrt, g_qa, g_kva, w2t, wkn, wvmt):
    bsz, seq, d = x.shape
    ts = PROJ_TILE
    grid = (bsz, seq // ts)
    row = lambda b, s: (b, s, 0)
    col = lambda b, s: (b, 0, s)
    out_shape = (
        jax.ShapeDtypeStruct((bsz, seq, SB_WIDTH), _BF16),
        jax.ShapeDtypeStruct((bsz, seq, SB_WIDTH), _BF16),
        jax.ShapeDtypeStruct((bsz, SB_WIDTH, seq), _BF16),
        jax.ShapeDtypeStruct((bsz, MLA_HEADS, MLA_QK_PAD, seq), _BF16),
        jax.ShapeDtypeStruct((bsz, seq, MLA_HEADS * MLA_QK_PAD), _BF16),
        jax.ShapeDtypeStruct((bsz, MLA_HEADS, V_EXT, seq), _BF16),
    )
    heads_col = lambda b, s: (b, 0, 0, s)
    return pl.pallas_call(
        _proj_kernel,
        grid=grid,
        in_specs=[
            pl.BlockSpec((1, ts, d), row),
            pl.BlockSpec((1, 1, ts), col),
            _full(invf.shape), _full(g_attn.shape), _full(w1.shape),
            _full(wvt.shape), _full(wkrt.shape), _full(g_qa.shape), _full(g_kva.shape),
            _full(w2t.shape), _full(wkn.shape), _full(wvmt.shape),
        ],
        out_specs=(
            pl.BlockSpec((1, ts, SB_WIDTH), row),
            pl.BlockSpec((1, ts, SB_WIDTH), row),
            pl.BlockSpec((1, SB_WIDTH, ts), col),
            pl.BlockSpec((1, MLA_HEADS, MLA_QK_PAD, ts), heads_col),
            pl.BlockSpec((1, ts, MLA_HEADS * MLA_QK_PAD), row),
            pl.BlockSpec((1, MLA_HEADS, V_EXT, ts), heads_col),
        ),
        out_shape=out_shape,
        compiler_params=pltpu.CompilerParams(
            dimension_semantics=("arbitrary", "arbitrary"), vmem_limit_bytes=VMEM_LIMIT),
        name="proj",
    )(x, pos_row, invf, g_attn, w1, wvt, wkrt, g_qa, g_kva, w2t, wkn, wvmt)


def _sb_parts(q_ref, k_ref, vt_ref, tri_ref, o_ref, acc_ref, mass_ref, minmass_ref, nls_ref,
              spb_ref, ab_ref):
    t = SB_TILE
    tri = tri_ref[...]
    sub = lax.broadcasted_iota(jnp.int32, (LANES, t), 0)
    first = sub < SB_HEAD_DIM

    def strict_mask():
        key = lax.broadcasted_iota(jnp.int32, (t, t), 0)
        qry = lax.broadcasted_iota(jnp.int32, (t, t), 1)
        return key < qry

    def stage_softplus(c, qt, j, masked):
        start = pl.multiple_of(j * t, t)
        z = _dot(k_ref[0, pl.ds(start, t), :], qt)
        if masked:
            z = jnp.where(strict_mask(), z, SB_MASKED_SCORE)
        zb = z.astype(_BF16)
        pos = jnp.maximum(zb, 0.0)
        neg = zb - pos
        lg = jnp.log2((1.0 + jnp.exp2(neg - pos)).astype(_F32)).astype(_BF16)
        sp = pos + lg
        spb_ref[c] = sp
        nls_ref[c] = neg - lg
        return sp[:1, :].astype(_F32)

    def stage_weights(c, sp_row0):
        excl = _dot(tri, spb_ref[c])
        ab_ref[c] = jnp.exp2(nls_ref[c] - excl.astype(_BF16))
        return excl[:1, :] + sp_row0

    def stage_pv(c, j):
        start = pl.multiple_of(j * t, t)
        return _dot(vt_ref[0, :, pl.ds(start, t)], ab_ref[c])

    def load_qt(i):
        start = pl.multiple_of(i * t, t)
        qt = q_ref[0, pl.ds(start, t), :].astype(_F32).T
        zero = jnp.zeros_like(qt)
        return (jnp.where(first, qt, zero).astype(_BF16), jnp.where(first, zero, qt).astype(_BF16))

    def straight_line(i):
        state = {}

        def begin():
            qts = load_qt(i)
            prev = jnp.maximum(i - 1, 0)
            state["chains"] = [(qts[0], i, True), (qts[0], prev, False),
                               (qts[1], i, True), (qts[1], prev, False)]
            state["rows0"], state["tots"] = {}, {}

        def softplus_step(c):
            def step():
                state["rows0"][c] = stage_softplus(c, *state["chains"][c])
            return step

        def weights_step(c):
            def step():
                state["tots"][c] = stage_weights(c, state["rows0"][c])
            return step

        def combine_step(hh):
            def step():
                d = 2 * hh
                tots, chains = state["tots"], state["chains"]
                pv_diag, pv_prev = stage_pv(d, chains[d][1]), stage_pv(d + 1, chains[d + 1][1])
                has_prev = jnp.where(i > 0, 1.0, 0.0).astype(_F32)
                acc = pv_diag + (jnp.exp2(-tots[d]) * has_prev) * pv_prev
                mass = tots[d] + has_prev * tots[d + 1]
                acc_ref[hh] = acc
                mass_ref[hh] = mass
                state["acc", hh] = acc
                minmass_ref[...] = jnp.minimum(minmass_ref[...], mass)
                if hh == 1:
                    out_t = jnp.where(first, state["acc", 0], acc)
                    o_ref[0, pl.ds(pl.multiple_of(i * t, t), t), :] = out_t.T
            return step

        return dict(begin=begin, softplus=[softplus_step(c) for c in range(4)],
                    weights=[weights_step(c) for c in range(4)],
                    combine=[combine_step(hh) for hh in range(2)])

    def sweep_rest(i):
        def min_mass():
            return jnp.min(jnp.minimum(mass_ref[0], mass_ref[1]))

        def cond(c):
            j, mm = c
            return jnp.logical_and(j >= 0, mm < SB_DEAD_MASS)

        def body(c):
            j, _ = c
            qts = load_qt(i)
            for hh in range(2):
                tot = stage_weights(hh, stage_softplus(hh, qts[hh], j, False))
                mass = mass_ref[hh]
                acc_ref[hh] += jnp.exp2(-mass) * stage_pv(hh, j)
                mass_ref[hh] = mass + tot
            return j - 1, min_mass()

        lax.while_loop(cond, body, (i - 2, min_mass()))
        out_t = jnp.where(first, acc_ref[0], acc_ref[1])
        o_ref[0, pl.ds(pl.multiple_of(i * t, t), t), :] = out_t.T

    def finish_all(n_tiles):
        @pl.when(jnp.min(minmass_ref[...]) < SB_DEAD_MASS)
        def _():
            def one(i, carry):
                sb = straight_line(i)
                for step in [sb["begin"]] + sb["softplus"] + sb["weights"] + sb["combine"]:
                    step()
                sweep_rest(i)
                return carry
            lax.fori_loop(0, n_tiles, one, 0)

    return straight_line, finish_all


def _mla_parts(qt_ref, k_ref, vt_ref, bias_ref, o_ref, acc_ref, m_ref):
    t = MLA_TILE
    n_q = k_ref.shape[1] // t

    def scores_into(dst_ref, i, j):
        i = jnp.minimum(i, n_q - 1)
        j = jnp.minimum(j, n_q - 1)
        qs = pl.multiple_of(i * t, t)
        ks = pl.multiple_of(j * t, t)
        dst_ref[...] = _dot(k_ref[0, pl.ds(ks, t), :], qt_ref[0, 0, :, pl.ds(qs, t)])

    def soften(src_ref, i, diagonal):
        s = src_ref[...]
        if diagonal:
            s = s + bias_ref[...]
        m_old = m_ref[i]
        m_new = jnp.maximum(m_old, jnp.max(s, axis=0, keepdims=True))
        p = jnp.exp2(s - m_new).astype(_BF16)
        if not diagonal:
            m_ref[i] = m_new
        return p, jnp.exp2(m_old - m_new)

    def accumulate(p, alpha, i, j, diagonal):
        ks = pl.multiple_of(j * t, t)
        vt = vt_ref[0, 0, :, pl.ds(ks, t)]
        acc = alpha * acc_ref[i] + _dot(vt, p)
        if diagonal:
            qs = pl.multiple_of(i * t, t)
            o_ref[0, pl.ds(qs, t), :] = (acc[:V_DIM, :] / acc[V_DIM:V_DIM + 1, :]).T
        else:
            acc_ref[i] = acc

    return scores_into, soften, accumulate


def _attn_kernel(qt_ref, km_ref, vtm_ref, bias_ref, qs_ref, ks_ref, vts_ref, tri_ref,
                 om_ref, os_ref, acc_ref, m_ref, sa_ref, sb_ref, sacc_ref, smass_ref, minmass_ref,
                 nls_ref, spb_ref, ab_ref):
    scores_into, soften, accumulate = _mla_parts(qt_ref, km_ref, vtm_ref, bias_ref, om_ref, acc_ref, m_ref)
    sb_straight, sb_finish_all = _sb_parts(qs_ref, ks_ref, vts_ref, tri_ref, os_ref, sacc_ref,
                                           smass_ref, minmass_ref, nls_ref, spb_ref, ab_ref)
    n_q = km_ref.shape[1] // MLA_TILE
    n_off = n_q * (n_q - 1) // 2
    n_sb = qs_ref.shape[1] // SB_TILE

    acc_ref[...] = jnp.zeros_like(acc_ref)
    m_ref[...] = jnp.full_like(m_ref, jnp.finfo(_F32).min)
    minmass_ref[...] = jnp.full_like(minmass_ref, jnp.finfo(_F32).max)

    def advance(i, j):
        wrap = j + 1 >= i
        return jnp.where(wrap, i + 1, i), jnp.where(wrap, 0, j + 1)

    def tile_steps(src_ref, i, j, diagonal):
        state = {}

        def soften_step():
            state["p"], state["alpha"] = soften(src_ref, i, diagonal)

        def accumulate_step():
            accumulate(state["p"], state["alpha"], i, j, diagonal)

        return [soften_step, accumulate_step]

    def update(src_ref, i, j, diagonal):
        for step in tile_steps(src_ref, i, j, diagonal):
            step()

    def off_steps(n, carry):
        i, j = carry
        i1, j1 = advance(i, j)
        i2, j2 = advance(i1, j1)
        return ([lambda: scores_into(sb_ref, i1, j1)] + tile_steps(sa_ref, i, j, False)
                + [lambda: scores_into(sa_ref, i2, j2)]
                + tile_steps(sb_ref, i1, j1, False)), (i2, j2)

    def diag_steps(n, carry):
        i = 2 * n
        return ([lambda: scores_into(sb_ref, i + 1, i + 1)] + tile_steps(sa_ref, i, i, True)
                + [lambda: scores_into(sa_ref, i + 2, i + 2)]
                + tile_steps(sb_ref, i + 1, i + 1, True)), carry

    def plain(mla_steps):
        def body(n, carry):
            steps, carry = mla_steps(n, carry)
            for step in steps:
                step()
            return carry
        return body

    def with_sb(first_sb_tile, mla_steps):
        def body(n, carry):
            mla, carry = mla_steps(n, carry)
            sb = sb_straight(first_sb_tile + n)
            sp, wt, cb = sb["softplus"], sb["weights"], sb["combine"]
            order = [sb["begin"], mla[0], sp[0], mla[1], mla[2], sp[1], wt[0], sp[2], wt[1], mla[3],
                     sp[3], cb[0], wt[2], mla[4], mla[5], wt[3], cb[1]]
            for step in order:
                step()
            return carry
        return body

    sb_done = 0
    trips = n_off // 2
    if n_off:
        scores_into(sa_ref, 1, 0)
        fused = min(trips, n_sb - sb_done)
        carry = (jnp.int32(1), jnp.int32(0))
        carry = lax.fori_loop(0, fused, with_sb(sb_done, off_steps), carry)
        carry = lax.fori_loop(fused, trips, plain(off_steps), carry)
        sb_done += fused
        if n_off % 2:
            update(sa_ref, carry[0], carry[1], False)

    trips = n_q // 2
    scores_into(sa_ref, 0, 0)
    fused = min(trips, n_sb - sb_done)
    lax.fori_loop(0, fused, with_sb(sb_done, diag_steps), 0)
    lax.fori_loop(fused, trips, plain(diag_steps), 0)
    sb_done += fused
    if n_q % 2:
        update(sa_ref, n_q - 1, n_q - 1, True)

    def sb_only(n, carry):
        sb = sb_straight(n)
        for step in [sb["begin"]] + sb["softplus"] + sb["weights"] + sb["combine"]:
            step()
        return carry

    lax.fori_loop(sb_done, n_sb, sb_only, 0)
    sb_finish_all(n_sb)


def _attention(qtcat, kcat, vmt, bias, qsb, ksb, vtsb, tri):
    bsz, seq, _ = kcat.shape
    assert MLA_HEADS == SB_WIDTH // LANES
    t, ts = MLA_TILE, SB_TILE
    head_rows = lambda b, h: (b, h, 0, 0)
    lane_block = lambda b, h: (b, 0, h)
    return pl.pallas_call(
        _attn_kernel,
        grid=(bsz, MLA_HEADS),
        in_specs=[
            pl.BlockSpec((1, 1, MLA_QK_PAD, seq), head_rows),
            pl.BlockSpec((1, seq, MLA_QK_PAD), lane_block),
            pl.BlockSpec((1, 1, V_EXT, seq), head_rows),
            pl.BlockSpec((t, t), lambda b, h: (0, 0)),
            pl.BlockSpec((1, seq, LANES), lane_block),
            pl.BlockSpec((1, seq, LANES), lane_block),
            pl.BlockSpec((1, LANES, seq), lambda b, h: (b, h, 0)),
            pl.BlockSpec((ts, ts), lambda b, h: (0, 0)),
        ],
        out_specs=(pl.BlockSpec((1, seq, V_DIM), lane_block),
                   pl.BlockSpec((1, seq, LANES), lane_block)),
        out_shape=(jax.ShapeDtypeStruct((bsz, seq, MLA_WIDTH), _F32),
                   jax.ShapeDtypeStruct((bsz, seq, SB_WIDTH), _F32)),
        scratch_shapes=[pltpu.VMEM((seq // t, V_EXT, t), _F32), pltpu.VMEM((seq // t, 1, t), _F32),
                        pltpu.VMEM((t, t), _F32), pltpu.VMEM((t, t), _F32),
                        pltpu.VMEM((2, LANES, ts), _F32), pltpu.VMEM((2, 1, ts), _F32),
                        pltpu.VMEM((1, ts), _F32),
                        pltpu.VMEM((4, ts, ts), _BF16), pltpu.VMEM((4, ts, ts), _BF16),
                        pltpu.VMEM((4, ts, ts), _BF16)],
        compiler_params=pltpu.CompilerParams(
            dimension_semantics=("arbitrary", "arbitrary"), vmem_limit_bytes=VMEM_LIMIT),
        name="attention",
    )(qtcat, kcat, vmt, bias, qsb, ksb, vtsb, tri)


def _post_kernel(x_ref, osb_ref, omla_ref, g_sb_ref, g_mla_ref, wo_ref, g_mlp_ref, wup_ref,
                 wdown_ref, g_out_ref, out_ref, *, apply_out_norm):
    ms = _rms(osb_ref[...], g_sb_ref[...]).astype(_BF16)
    mm = _rms(omla_ref[...], g_mla_ref[...]).astype(_BF16)
    h1 = x_ref[...] + _dot(jnp.concatenate([ms, mm], axis=1), wo_ref[...])
    v = _rms(h1, g_mlp_ref[...]).astype(_BF16)
    hid = jnp.square(jnp.maximum(_dot(v, wup_ref[...]), 0.0)).astype(_BF16)
    h2 = h1 + _dot(hid, wdown_ref[...])
    out_ref[...] = _rms(h2, g_out_ref[...]) if apply_out_norm else h2


def _post(x2, osb2, omla2, g_sb, g_mla, wo, g_mlp, wup, wdown, g_out, apply_out_norm):
    n, d = x2.shape
    tm = POST_TILE
    once = pl.Buffered(1)
    rows = lambda w: pl.BlockSpec((tm, w), lambda r: (r, 0))
    const = lambda a: pl.BlockSpec(a.shape, lambda r: (0, 0), pipeline_mode=once)
    return pl.pallas_call(
        functools.partial(_post_kernel, apply_out_norm=apply_out_norm),
        grid=(n // tm,),
        in_specs=[rows(d), rows(SB_WIDTH), rows(MLA_WIDTH), const(g_sb), const(g_mla), const(wo),
                  const(g_mlp), const(wup), const(wdown), const(g_out)],
        out_specs=rows(d),
        out_shape=jax.ShapeDtypeStruct((n, d), _F32),
        compiler_params=pltpu.CompilerParams(
            dimension_semantics=("arbitrary",), vmem_limit_bytes=VMEM_LIMIT),
        name="post",
    )(x2, osb2, omla2, g_sb, g_mla, wo, g_mlp, wup, wdown, g_out)


def _swap_halves(w):
    half = QK_ROPE // 2
    return jnp.concatenate([w[..., half:], w[..., :half]], axis=-1)


def _layer_weights(w_in, w_q_b, w_kv_b):
    d = w_in.shape[0]
    s2, s3 = 2 * SB_WIDTH, 3 * SB_WIDTH
    s5 = s3 + Q_LORA + KV_LORA
    w1 = jnp.concatenate([w_in[:, :s2], w_in[:, s3:s5]], axis=1).astype(_BF16)
    wvt = w_in[:, s2:s3].T.astype(_BF16)
    wkr = w_in[:, s5:]
    pad = jnp.zeros((d, LANES - QK_ROPE), w_in.dtype)
    wkrt = jnp.concatenate([wkr, pad, _swap_halves(wkr), pad], axis=1).T.astype(_BF16)
    wqb = w_q_b.reshape(Q_LORA, MLA_HEADS, QK_NOPE + QK_ROPE)
    qn = wqb[:, :, :QK_NOPE].reshape(Q_LORA, MLA_HEADS * QK_NOPE)
    qr = wqb[:, :, QK_NOPE:]
    qpad = jnp.zeros((Q_LORA, MLA_HEADS, LANES - QK_ROPE), w_q_b.dtype)
    qr_pad = jnp.concatenate([qr, qpad], axis=-1).reshape(Q_LORA, MLA_HEADS * LANES)
    qrr_pad = jnp.concatenate([_swap_halves(qr), qpad], axis=-1).reshape(Q_LORA, MLA_HEADS * LANES)
    w2t = jnp.concatenate([qn, qr_pad, qrr_pad], axis=1).T.astype(_BF16)
    wkvb = w_kv_b.reshape(KV_LORA, MLA_HEADS, QK_NOPE + V_DIM)
    wkn = wkvb[:, :, :QK_NOPE].reshape(KV_LORA, MLA_HEADS * QK_NOPE).astype(_BF16)
    wvmt = wkvb[:, :, QK_NOPE:].reshape(KV_LORA, MLA_HEADS * V_DIM).T.astype(_BF16)
    return w1, wvt, wkrt, w2t, wkn, wvmt


def kernel(x, positions, attn_norm_g, w_in, q_a_norm_g, w_q_b, kv_a_norm_g, w_kv_b, sb_out_norm_g,
           mla_out_norm_g, w_o, mlp_norm_g, w_up, w_down, final_norm_g):
    bsz, seq, d = x.shape
    depth = w_in.shape[0]
    assert seq % PROJ_TILE == 0 and seq % SB_TILE == 0 and seq % MLA_TILE == 0
    assert (bsz * seq) % POST_TILE == 0

    half = QK_ROPE // 2
    inv_freq = ROPE_BASE ** (-jnp.arange(half, dtype=_F32) / half)
    invf = inv_freq[:, None]
    pos_row = positions[:, None, :]
    r = lax.broadcasted_iota(jnp.int32, (SB_TILE, SB_TILE), 0)
    c = lax.broadcasted_iota(jnp.int32, (SB_TILE, SB_TILE), 1)
    tri = (c > r).astype(_BF16)
    key = lax.broadcasted_iota(jnp.int32, (MLA_TILE, MLA_TILE), 0)
    qry = lax.broadcasted_iota(jnp.int32, (MLA_TILE, MLA_TILE), 1)
    bias = jnp.where(key <= qry, 0.0, MLA_MASK_BIAS).astype(_F32)

    row = lambda g: g[None, :]
    h = x
    for l in range(depth):
        w1, wvt, wkrt, w2t, wkn, wvmt = _layer_weights(w_in[l], w_q_b[l], w_kv_b[l])
        qsb, ksb, vtsb, qtcat, kcat, vmt = _proj(
            h, pos_row, invf, row(attn_norm_g[l]), w1, wvt, wkrt, row(q_a_norm_g[l]),
            row(kv_a_norm_g[l]), w2t, wkn, wvmt)
        omla, osb = _attention(qtcat, kcat, vmt, bias, qsb, ksb, vtsb, tri)
        last = l == depth - 1
        h = _post(h.reshape(bsz * seq, d), osb.reshape(bsz * seq, SB_WIDTH),
                  omla.reshape(bsz * seq, MLA_WIDTH), row(sb_out_norm_g[l]),
                  row(mla_out_norm_g[l]), w_o[l].astype(_BF16), row(mlp_norm_g[l]),
                  w_up[l].astype(_BF16), w_down[l].astype(_BF16), row(final_norm_g),
                  last).reshape(bsz, seq, d)
    return h
```

```python
import functools

import jax
import jax.numpy as jnp
from jax import lax
from jax.experimental import pallas as pl
from jax.experimental.pallas import tpu as pltpu

SB_HEADS = 8
SB_HEAD_DIM = 64
SB_WIDTH = SB_HEADS * SB_HEAD_DIM
SB_SCALE = SB_HEAD_DIM ** -0.5
MLA_HEADS = 4
QK_NOPE = 128
QK_ROPE = 64
V_DIM = 128
Q_LORA = 256
KV_LORA = 128
MLA_WIDTH = MLA_HEADS * V_DIM
MLA_SCALE = (QK_NOPE + QK_ROPE) ** -0.5
LOG2_E = 1.4426950408889634
ROPE_BASE = 10000.0
NORM_EPS = 1e-6

LANES = 128
MLA_QK_PAD = 256
V_EXT = V_DIM + 16
PROJ_TILE = 512
SB_TILE = 256
MLA_TILE = 512
POST_TILE = 512
VMEM_LIMIT = 56 * 1024 * 1024
SB_DEAD_MASS = 158.0
SB_MASKED_SCORE = -1e30
MLA_MASK_BIAS = -1e30

_F32 = jnp.float32
_BF16 = jnp.bfloat16


def _dot(a, b):
    return jnp.dot(a, b, preferred_element_type=_F32)


def _dot_nt(a, b):
    return lax.dot_general(a, b, (((1,), (1,)), ((), ())), preferred_element_type=_F32)


def _rms(x, g):
    y = x * lax.rsqrt(jnp.mean(x * x, axis=-1, keepdims=True) + NORM_EPS)
    return y * g


def _proj_kernel(x_ref, pos_ref, invf_ref, g_attn_ref, w1_ref, wvt_ref, wkrt_ref,
                 g_qa_ref, g_kva_ref, w2t_ref, wkn_ref, wvmt_ref,
                 qsb_ref, ksb_ref, vtsb_ref, qtcat_ref, kcat_ref, vmt_ref):
    u = _rms(x_ref[0], g_attn_ref[...]).astype(_BF16)
    p1 = _dot(u, w1_ref[...])
    qsb_ref[0] = (p1[:, :SB_WIDTH] * (SB_SCALE * LOG2_E)).astype(_BF16)
    ksb_ref[0] = p1[:, SB_WIDTH:2 * SB_WIDTH].astype(_BF16)
    vtsb_ref[0] = _dot_nt(wvt_ref[...], u).astype(_BF16)

    o = 2 * SB_WIDTH
    cqn = _rms(p1[:, o:o + Q_LORA], g_qa_ref[...]).astype(_BF16)
    o += Q_LORA
    ckvn = _rms(p1[:, o:o + KV_LORA], g_kva_ref[...]).astype(_BF16)

    ang = invf_ref[...] * pos_ref[0].astype(_F32)
    cos_h = jnp.cos(ang)
    sin_h = jnp.sin(ang)
    reps = LANES // QK_ROPE
    cost = jnp.concatenate([cos_h, cos_h] * reps, axis=0)
    sint = jnp.concatenate([-sin_h, sin_h] * reps, axis=0)

    krt = _dot_nt(wkrt_ref[...], u)
    kr_roped = (krt[:LANES] * cost + krt[LANES:] * sint).T.astype(_BF16)

    p2t = _dot_nt(w2t_ref[...], cqn)
    kn = _dot(ckvn, wkn_ref[...]).astype(_BF16)
    hw = MLA_HEADS * LANES
    for h in range(MLA_HEADS):
        sl = slice(h * LANES, (h + 1) * LANES)
        qn = p2t[sl] * (MLA_SCALE * LOG2_E)
        qr = (p2t[hw + h * LANES:hw + (h + 1) * LANES] * cost
              + p2t[2 * hw + h * LANES:2 * hw + (h + 1) * LANES] * sint) * (MLA_SCALE * LOG2_E)
        qtcat_ref[0, h, :LANES, :] = qn.astype(_BF16)
        qtcat_ref[0, h, LANES:, :] = qr.astype(_BF16)
        kcat_ref[0, :, h * MLA_QK_PAD:h * MLA_QK_PAD + LANES] = kn[:, sl]
        kcat_ref[0, :, h * MLA_QK_PAD + LANES:(h + 1) * MLA_QK_PAD] = kr_roped
    vt = _dot_nt(wvmt_ref[...], ckvn).astype(_BF16)
    ts = vt.shape[1]
    ones_row = (lax.broadcasted_iota(jnp.int32, (V_EXT - V_DIM, ts), 0) == 0).astype(_BF16)
    for h in range(MLA_HEADS):
        vmt_ref[0, h, :V_DIM, :] = vt[h * V_DIM:(h + 1) * V_DIM]
        vmt_ref[0, h, V_DIM:, :] = ones_row


def _full(shape):
    return pl.BlockSpec(shape, lambda *_: (0,) * len(shape))


def _proj(x, pos_row, invf, g_attn, w1, wvt, wkrt, g_qa, g_kva, w2t, wkn, wvmt):
    bsz, seq, d = x.shape
    ts = PROJ_TILE
    grid = (bsz, seq // ts)
    row = lambda b, s: (b, s, 0)
    col = lambda b, s: (b, 0, s)
    out_shape = (
        jax.ShapeDtypeStruct((bsz, seq, SB_WIDTH), _BF16),
        jax.ShapeDtypeStruct((bsz, seq, SB_WIDTH), _BF16),
        jax.ShapeDtypeStruct((bsz, SB_WIDTH, seq), _BF16),
        jax.ShapeDtypeStruct((bsz, MLA_HEADS, MLA_QK_PAD, seq), _BF16),
        jax.ShapeDtypeStruct((bsz, seq, MLA_HEADS * MLA_QK_PAD), _BF16),
        jax.ShapeDtypeStruct((bsz, MLA_HEADS, V_EXT, seq), _BF16),
    )
    heads_col = lambda b, s: (b, 0, 0, s)
    return pl.pallas_call(
        _proj_kernel,
        grid=grid,
        in_specs=[
            pl.BlockSpec((1, ts, d), row),
            pl.BlockSpec((1, 1, ts), col),
            _full(invf.shape), _full(g_attn.shape), _full(w1.shape),
            _full(wvt.shape), _full(wkrt.shape), _full(g_qa.shape), _full(g_kva.shape),
            _full(w2t.shape), _full(wkn.shape), _full(wvmt.shape),
        ],
        out_specs=(
            pl.BlockSpec((1, ts, SB_WIDTH), row),
            pl.BlockSpec((1, ts, SB_WIDTH), row),
            pl.BlockSpec((1, SB_WIDTH, ts), col),
            pl.BlockSpec((1, MLA_HEADS, MLA_QK_PAD, ts), heads_col),
            pl.BlockSpec((1, ts, MLA_HEADS * MLA_QK_PAD), row),
            pl.BlockSpec((1, MLA_HEADS, V_EXT, ts), heads_col),
        ),
        out_shape=out_shape,
        compiler_params=pltpu.CompilerParams(
            dimension_semantics=("arbitrary", "arbitrary"), vmem_limit_bytes=VMEM_LIMIT),
        name="proj",
    )(x, pos_row, invf, g_attn, w1, wvt, wkrt, g_qa, g_kva, w2t, wkn, wvmt)


def _sb_parts(q_ref, k_ref, vt_ref, tri_ref, o_ref, acc_ref, mass_ref, minmass_ref, nls_ref,
              spb_ref, ab_ref):
    t = SB_TILE
    tri = tri_ref[...]
    sub = lax.broadcasted_iota(jnp.int32, (LANES, t), 0)
    first = sub < SB_HEAD_DIM

    def strict_mask():
        key = lax.broadcasted_iota(jnp.int32, (t, t), 0)
        qry = lax.broadcasted_iota(jnp.int32, (t, t), 1)
        return key < qry

    def stage_softplus(c, qt, j, masked):
        start = pl.multiple_of(j * t, t)
        z = _dot(k_ref[0, pl.ds(start, t), :], qt)
        if masked:
            z = jnp.where(strict_mask(), z, SB_MASKED_SCORE)
        zb = z.astype(_BF16)
        pos = jnp.maximum(zb, 0.0)
        neg = zb - pos
        lg = jnp.log2((1.0 + jnp.exp2(neg - pos)).astype(_F32)).astype(_BF16)
        sp = pos + lg
        spb_ref[c] = sp
        nls_ref[c] = neg - lg
        return sp[:1, :].astype(_F32)

    def stage_weights(c, sp_row0):
        excl = _dot(tri, spb_ref[c])
        ab_ref[c] = jnp.exp2(nls_ref[c] - excl.astype(_BF16))
        return excl[:1, :] + sp_row0

    def stage_pv(c, j):
        start = pl.multiple_of(j * t, t)
        return _dot(vt_ref[0, :, pl.ds(start, t)], ab_ref[c])

    def load_qt(i):
        start = pl.multiple_of(i * t, t)
        qt = q_ref[0, pl.ds(start, t), :].astype(_F32).T
        zero = jnp.zeros_like(qt)
        return (jnp.where(first, qt, zero).astype(_BF16), jnp.where(first, zero, qt).astype(_BF16))

    def straight_line(i):
        state = {}

        def begin():
            qts = load_qt(i)
            prev = jnp.maximum(i - 1, 0)
            state["chains"] = [(qts[0], i, True), (qts[0], prev, False),
                               (qts[1], i, True), (qts[1], prev, False)]
            state["rows0"], state["tots"] = {}, {}

        def softplus_step(c):
            def step():
                state["rows0"][c] = stage_softplus(c, *state["chains"][c])
            return step

        def weights_step(c):
            def step():
                state["tots"][c] = stage_weights(c, state["rows0"][c])
            return step

        def combine_step(hh):
            def step():
                d = 2 * hh
                tots, chains = state["tots"], state["chains"]
                pv_diag, pv_prev = stage_pv(d, chains[d][1]), stage_pv(d + 1, chains[d + 1][1])
                has_prev = jnp.where(i > 0, 1.0, 0.0).astype(_F32)
                acc = pv_diag + (jnp.exp2(-tots[d]) * has_prev) * pv_prev
                mass = tots[d] + has_prev * tots[d + 1]
                acc_ref[hh] = acc
                mass_ref[hh] = mass
                state["acc", hh] = acc
                minmass_ref[...] = jnp.minimum(
                    minmass_ref[...], jnp.where(i >= 2, mass, jnp.finfo(_F32).max))
                if hh == 1:
                    out_t = jnp.where(first, state["acc", 0], acc)
                    o_ref[0, pl.ds(pl.multiple_of(i * t, t), t), :] = out_t.T
            return step

        return dict(begin=begin, softplus=[softplus_step(c) for c in range(4)],
                    weights=[weights_step(c) for c in range(4)],
                    combine=[combine_step(hh) for hh in range(2)])

    def sweep_rest(i):
        def min_mass():
            return jnp.min(jnp.minimum(mass_ref[0], mass_ref[1]))

        def cond(c):
            j, mm = c
            return jnp.logical_and(j >= 0, mm < SB_DEAD_MASS)

        def body(c):
            j, _ = c
            qts = load_qt(i)
            for hh in range(2):
                tot = stage_weights(hh, stage_softplus(hh, qts[hh], j, False))
                mass = mass_ref[hh]
                acc_ref[hh] += jnp.exp2(-mass) * stage_pv(hh, j)
                mass_ref[hh] = mass + tot
            return j - 1, min_mass()

        lax.while_loop(cond, body, (i - 2, min_mass()))
        out_t = jnp.where(first, acc_ref[0], acc_ref[1])
        o_ref[0, pl.ds(pl.multiple_of(i * t, t), t), :] = out_t.T

    def finish_all(n_tiles):
        @pl.when(jnp.min(minmass_ref[...]) < SB_DEAD_MASS)
        def _():
            def one(i, carry):
                sb = straight_line(i)
                for step in [sb["begin"]] + sb["softplus"] + sb["weights"] + sb["combine"]:
                    step()
                sweep_rest(i)
                return carry
            lax.fori_loop(0, n_tiles, one, 0)

    return straight_line, finish_all


def _mla_parts(qt_ref, k_ref, vt_ref, bias_ref, o_ref, acc_ref, m_ref):
    t = MLA_TILE
    n_q = k_ref.shape[1] // t

    def scores_into(dst_ref, i, j):
        i = jnp.minimum(i, n_q - 1)
        j = jnp.minimum(j, n_q - 1)
        qs = pl.multiple_of(i * t, t)
        ks = pl.multiple_of(j * t, t)
        dst_ref[...] = _dot(k_ref[0, pl.ds(ks, t), :], qt_ref[0, 0, :, pl.ds(qs, t)])

    def soften(src_ref, i, diagonal):
        s = src_ref[...]
        if diagonal:
            s = s + bias_ref[...]
        m_old = m_ref[i]
        m_new = jnp.maximum(m_old, jnp.max(s, axis=0, keepdims=True))
        p = jnp.exp2(s - m_new).astype(_BF16)
        if not diagonal:
            m_ref[i] = m_new
        return p, jnp.exp2(m_old - m_new)

    def accumulate(p, alpha, i, j, diagonal):
        ks = pl.multiple_of(j * t, t)
        vt = vt_ref[0, 0, :, pl.ds(ks, t)]
        acc = alpha * acc_ref[i] + _dot(vt, p)
        if diagonal:
            qs = pl.multiple_of(i * t, t)
            o_ref[0, pl.ds(qs, t), :] = (acc[:V_DIM, :] / acc[V_DIM:V_DIM + 1, :]).T
        else:
            acc_ref[i] = acc

    return scores_into, soften, accumulate


def _attn_kernel(qt_ref, km_ref, vtm_ref, bias_ref, qs_ref, ks_ref, vts_ref, tri_ref,
                 om_ref, os_ref, acc_ref, m_ref, sa_ref, sb_ref, sacc_ref, smass_ref, minmass_ref,
                 nls_ref, spb_ref, ab_ref):
    scores_into, soften, accumulate = _mla_parts(qt_ref, km_ref, vtm_ref, bias_ref, om_ref, acc_ref, m_ref)
    sb_straight, sb_finish_all = _sb_parts(qs_ref, ks_ref, vts_ref, tri_ref, os_ref, sacc_ref,
                                           smass_ref, minmass_ref, nls_ref, spb_ref, ab_ref)
    n_q = km_ref.shape[1] // MLA_TILE
    n_off = n_q * (n_q - 1) // 2
    n_sb = qs_ref.shape[1] // SB_TILE

    acc_ref[...] = jnp.zeros_like(acc_ref)
    m_ref[...] = jnp.full_like(m_ref, jnp.finfo(_F32).min)
    minmass_ref[...] = jnp.full_like(minmass_ref, jnp.finfo(_F32).max)

    def advance(i, j):
        wrap = j + 1 >= i
        return jnp.where(wrap, i + 1, i), jnp.where(wrap, 0, j + 1)

    def tile_steps(src_ref, i, j, diagonal):
        state = {}

        def soften_step():
            state["p"], state["alpha"] = soften(src_ref, i, diagonal)

        def accumulate_step():
            accumulate(state["p"], state["alpha"], i, j, diagonal)

        return [soften_step, accumulate_step]

    def update(src_ref, i, j, diagonal):
        for step in tile_steps(src_ref, i, j, diagonal):
            step()

    def off_steps(n, carry):
        i, j = carry
        i1, j1 = advance(i, j)
        i2, j2 = advance(i1, j1)
        return ([lambda: scores_into(sb_ref, i1, j1)] + tile_steps(sa_ref, i, j, False)
                + [lambda: scores_into(sa_ref, i2, j2)]
                + tile_steps(sb_ref, i1, j1, False)), (i2, j2)

    def diag_steps(n, carry):
        i = 2 * n
        return ([lambda: scores_into(sb_ref, i + 1, i + 1)] + tile_steps(sa_ref, i, i, True)
                + [lambda: scores_into(sa_ref, i + 2, i + 2)]
                + tile_steps(sb_ref, i + 1, i + 1, True)), carry

    def plain(mla_steps):
        def body(n, carry):
            steps, carry = mla_steps(n, carry)
            for step in steps:
                step()
            return carry
        return body

    def with_sb(first_sb_tile, mla_steps):
        def body(n, carry):
            mla, carry = mla_steps(n, carry)
            sb = sb_straight(first_sb_tile + n)
            sp, wt, cb = sb["softplus"], sb["weights"], sb["combine"]
            order = [sb["begin"], mla[0], sp[0], mla[1], mla[2], sp[1], wt[0], sp[2], wt[1], mla[3],
                     sp[3], cb[0], wt[2], mla[4], mla[5], wt[3], cb[1]]
            for step in order:
                step()
            return carry
        return body

    sb_done = 0
    trips = n_off // 2
    if n_off:
        scores_into(sa_ref, 1, 0)
        fused = min(trips, n_sb - sb_done)
        carry = (jnp.int32(1), jnp.int32(0))
        carry = lax.fori_loop(0, fused, with_sb(sb_done, off_steps), carry)
        carry = lax.fori_loop(fused, trips, plain(off_steps), carry)
        sb_done += fused
        if n_off % 2:
            update(sa_ref, carry[0], carry[1], False)

    trips = n_q // 2
    scores_into(sa_ref, 0, 0)
    fused = min(trips, n_sb - sb_done)
    lax.fori_loop(0, fused, with_sb(sb_done, diag_steps), 0)
    lax.fori_loop(fused, trips, plain(diag_steps), 0)
    sb_done += fused
    if n_q % 2:
        update(sa_ref, n_q - 1, n_q - 1, True)

    def sb_only(n, carry):
        sb = sb_straight(n)
        for step in [sb["begin"]] + sb["softplus"] + sb["weights"] + sb["combine"]:
            step()
        return carry

    lax.fori_loop(sb_done, n_sb, sb_only, 0)
    sb_finish_all(n_sb)


def _attention(qtcat, kcat, vmt, bias, qsb, ksb, vtsb, tri):
    bsz, seq, _ = kcat.shape
    assert MLA_HEADS == SB_WIDTH // LANES
    t, ts = MLA_TILE, SB_TILE
    head_rows = lambda b, h: (b, h, 0, 0)
    lane_block = lambda b, h: (b, 0, h)
    return pl.pallas_call(
        _attn_kernel,
        grid=(bsz, MLA_HEADS),
        in_specs=[
            pl.BlockSpec((1, 1, MLA_QK_PAD, seq), head_rows),
            pl.BlockSpec((1, seq, MLA_QK_PAD), lane_block),
            pl.BlockSpec((1, 1, V_EXT, seq), head_rows),
            pl.BlockSpec((t, t), lambda b, h: (0, 0)),
            pl.BlockSpec((1, seq, LANES), lane_block),
            pl.BlockSpec((1, seq, LANES), lane_block),
            pl.BlockSpec((1, LANES, seq), lambda b, h: (b, h, 0)),
            pl.BlockSpec((ts, ts), lambda b, h: (0, 0)),
        ],
        out_specs=(pl.BlockSpec((1, seq, V_DIM), lane_block),
                   pl.BlockSpec((1, seq, LANES), lane_block)),
        out_shape=(jax.ShapeDtypeStruct((bsz, seq, MLA_WIDTH), _F32),
                   jax.ShapeDtypeStruct((bsz, seq, SB_WIDTH), _F32)),
        scratch_shapes=[pltpu.VMEM((seq // t, V_EXT, t), _F32), pltpu.VMEM((seq // t, 1, t), _F32),
                        pltpu.VMEM((t, t), _F32), pltpu.VMEM((t, t), _F32),
                        pltpu.VMEM((2, LANES, ts), _F32), pltpu.VMEM((2, 1, ts), _F32),
                        pltpu.VMEM((1, ts), _F32),
                        pltpu.VMEM((4, ts, ts), _BF16), pltpu.VMEM((4, ts, ts), _BF16),
                        pltpu.VMEM((4, ts, ts), _BF16)],
        compiler_params=pltpu.CompilerParams(
            dimension_semantics=("arbitrary", "arbitrary"), vmem_limit_bytes=VMEM_LIMIT),
        name="attention",
    )(qtcat, kcat, vmt, bias, qsb, ksb, vtsb, tri)


def _post_kernel(x_ref, osb_ref, omla_ref, g_sb_ref, g_mla_ref, wo_ref, g_mlp_ref, wup_ref,
                 wdown_ref, g_out_ref, out_ref, *, apply_out_norm):
    ms = _rms(osb_ref[...], g_sb_ref[...]).astype(_BF16)
    mm = _rms(omla_ref[...], g_mla_ref[...]).astype(_BF16)
    h1 = x_ref[...] + _dot(jnp.concatenate([ms, mm], axis=1), wo_ref[...])
    v = _rms(h1, g_mlp_ref[...]).astype(_BF16)
    hid = jnp.square(jnp.maximum(_dot(v, wup_ref[...]), 0.0)).astype(_BF16)
    h2 = h1 + _dot(hid, wdown_ref[...])
    out_ref[...] = _rms(h2, g_out_ref[...]) if apply_out_norm else h2


def _post(x2, osb2, omla2, g_sb, g_mla, wo, g_mlp, wup, wdown, g_out, apply_out_norm):
    n, d = x2.shape
    tm = POST_TILE
    once = pl.Buffered(1)
    rows = lambda w: pl.BlockSpec((tm, w), lambda r: (r, 0))
    const = lambda a: pl.BlockSpec(a.shape, lambda r: (0, 0), pipeline_mode=once)
    return pl.pallas_call(
        functools.partial(_post_kernel, apply_out_norm=apply_out_norm),
        grid=(n // tm,),
        in_specs=[rows(d), rows(SB_WIDTH), rows(MLA_WIDTH), const(g_sb), const(g_mla), const(wo),
                  const(g_mlp), const(wup), const(wdown), const(g_out)],
        out_specs=rows(d),
        out_shape=jax.ShapeDtypeStruct((n, d), _F32),
        compiler_params=pltpu.CompilerParams(
            dimension_semantics=("arbitrary",), vmem_limit_bytes=VMEM_LIMIT),
        name="post",
    )(x2, osb2, omla2, g_sb, g_mla, wo, g_mlp, wup, wdown, g_out)


def _swap_halves(w):
    half = QK_ROPE // 2
    return jnp.concatenate([w[..., half:], w[..., :half]], axis=-1)


def _layer_weights(w_in, w_q_b, w_kv_b):
    d = w_in.shape[0]
    s2, s3 = 2 * SB_WIDTH, 3 * SB_WIDTH
    s5 = s3 + Q_LORA + KV_LORA
    w1 = jnp.concatenate([w_in[:, :s2], w_in[:, s3:s5]], axis=1).astype(_BF16)
    wvt = w_in[:, s2:s3].T.astype(_BF16)
    wkr = w_in[:, s5:]
    pad = jnp.zeros((d, LANES - QK_ROPE), w_in.dtype)
    wkrt = jnp.concatenate([wkr, pad, _swap_halves(wkr), pad], axis=1).T.astype(_BF16)
    wqb = w_q_b.reshape(Q_LORA, MLA_HEADS, QK_NOPE + QK_ROPE)
    qn = wqb[:, :, :QK_NOPE].reshape(Q_LORA, MLA_HEADS * QK_NOPE)
    qr = wqb[:, :, QK_NOPE:]
    qpad = jnp.zeros((Q_LORA, MLA_HEADS, LANES - QK_ROPE), w_q_b.dtype)
    qr_pad = jnp.concatenate([qr, qpad], axis=-1).reshape(Q_LORA, MLA_HEADS * LANES)
    qrr_pad = jnp.concatenate([_swap_halves(qr), qpad], axis=-1).reshape(Q_LORA, MLA_HEADS * LANES)
    w2t = jnp.concatenate([qn, qr_pad, qrr_pad], axis=1).T.astype(_BF16)
    wkvb = w_kv_b.reshape(KV_LORA, MLA_HEADS, QK_NOPE + V_DIM)
    wkn = wkvb[:, :, :QK_NOPE].reshape(KV_LORA, MLA_HEADS * QK_NOPE).astype(_BF16)
    wvmt = wkvb[:, :, QK_NOPE:].reshape(KV_LORA, MLA_HEADS * V_DIM).T.astype(_BF16)
    return w1, wvt, wkrt, w2t, wkn, wvmt


def kernel(x, positions, attn_norm_g, w_in, q_a_norm_g, w_q_b, kv_a_norm_g, w_kv_b, sb_out_norm_g,
           mla_out_norm_g, w_o, mlp_norm_g, w_up, w_down, final_norm_g):
    bsz, seq, d = x.shape
    depth = w_in.shape[0]
    assert seq % PROJ_TILE == 0 and seq % SB_TILE == 0 and seq % MLA_TILE == 0
    assert (bsz * seq) % POST_TILE == 0

    half = QK_ROPE // 2
    inv_freq = ROPE_BASE ** (-jnp.arange(half, dtype=_F32) / half)
    invf = inv_freq[:, None]
    pos_row = positions[:, None, :]
    r = lax.broadcasted_iota(jnp.int32, (SB_TILE, SB_TILE), 0)
    c = lax.broadcasted_iota(jnp.int32, (SB_TILE, SB_TILE), 1)
    tri = (c > r).astype(_BF16)
    key = lax.broadcasted_iota(jnp.int32, (MLA_TILE, MLA_TILE), 0)
    qry = lax.broadcasted_iota(jnp.int32, (MLA_TILE, MLA_TILE), 1)
    bias = jnp.where(key <= qry, 0.0, MLA_MASK_BIAS).astype(_F32)

    row = lambda g: g[None, :]
    h = x
    for l in range(depth):
        w1, wvt, wkrt, w2t, wkn, wvmt = _layer_weights(w_in[l], w_q_b[l], w_kv_b[l])
        qsb, ksb, vtsb, qtcat, kcat, vmt = _proj(
            h, pos_row, invf, row(attn_norm_g[l]), w1, wvt, wkrt, row(q_a_norm_g[l]),
            row(kv_a_norm_g[l]), w2t, wkn, wvmt)
        omla, osb = _attention(qtcat, kcat, vmt, bias, qsb, ksb, vtsb, tri)
        last = l == depth - 1
        h = _post(h.reshape(bsz * seq, d), osb.reshape(bsz * seq, SB_WIDTH),
                  omla.reshape(bsz * seq, MLA_WIDTH), row(sb_out_norm_g[l]),
                  row(mla_out_norm_g[l]), w_o[l].astype(_BF16), row(mlp_norm_g[l]),
                  w_up[l].astype(_BF16), w_down[l].astype(_BF16), row(final_norm_g),
                  last).reshape(bsz, seq, d)
    return h
```

```python
import functools

import jax
import jax.numpy as jnp
from jax import lax
from jax.experimental import pallas as pl
from jax.experimental.pallas import tpu as pltpu

SB_HEADS = 8
SB_HEAD_DIM = 64
SB_WIDTH = SB_HEADS * SB_HEAD_DIM
SB_SCALE = SB_HEAD_DIM ** -0.5
MLA_HEADS = 4
QK_NOPE = 128
QK_ROPE = 64
V_DIM = 128
Q_LORA = 256
KV_LORA = 128
MLA_WIDTH = MLA_HEADS * V_DIM
MLA_SCALE = (QK_NOPE + QK_ROPE) ** -0.5
LOG2_E = 1.4426950408889634
ROPE_BASE = 10000.0
NORM_EPS = 1e-6

LANES = 128
MLA_QK_PAD = 256
V_EXT = V_DIM + 16
PROJ_TILE = 512
SB_TILE = 256
MLA_TILE = 512
POST_TILE = 512
VMEM_LIMIT = 56 * 1024 * 1024
SB_DEAD_MASS = 158.0
SB_MASKED_SCORE = -1e30
MLA_MASK_BIAS = -1e30

_F32 = jnp.float32
_BF16 = jnp.bfloat16


def _dot(a, b):
    return jnp.dot(a, b, preferred_element_type=_F32)


def _dot_nt(a, b):
    return lax.dot_general(a, b, (((1,), (1,)), ((), ())), preferred_element_type=_F32)


def _rms(x, g):
    y = x * lax.rsqrt(jnp.mean(x * x, axis=-1, keepdims=True) + NORM_EPS)
    return y * g


def _proj_kernel(x_ref, pos_ref, invf_ref, g_attn_ref, w1_ref, wvt_ref, wkrt_ref,
                 g_qa_ref, g_kva_ref, w2t_ref, wkn_ref, wvmt_ref,
                 qsb_ref, ksb_ref, vtsb_ref, qtcat_ref, kcat_ref, vmt_ref):
    u = _rms(x_ref[0], g_attn_ref[...]).astype(_BF16)
    p1 = _dot(u, w1_ref[...])
    qsb_ref[0] = (p1[:, :SB_WIDTH] * (SB_SCALE * LOG2_E)).astype(_BF16)
    ksb_ref[0] = p1[:, SB_WIDTH:2 * SB_WIDTH].astype(_BF16)
    vtsb_ref[0] = _dot_nt(wvt_ref[...], u).astype(_BF16)

    o = 2 * SB_WIDTH
    cqn = _rms(p1[:, o:o + Q_LORA], g_qa_ref[...]).astype(_BF16)
    o += Q_LORA
    ckvn = _rms(p1[:, o:o + KV_LORA], g_kva_ref[...]).astype(_BF16)

    ang = invf_ref[...] * pos_ref[0].astype(_F32)
    cos_h = jnp.cos(ang)
    sin_h = jnp.sin(ang)
    reps = LANES // QK_ROPE
    cost = jnp.concatenate([cos_h, cos_h] * reps, axis=0)
    sint = jnp.concatenate([-sin_h, sin_h] * reps, axis=0)

    krt = _dot_nt(wkrt_ref[...], u)
    kr_roped = (krt[:LANES] * cost + krt[LANES:] * sint).T.astype(_BF16)

    p2t = _dot_nt(w2t_ref[...], cqn)
    kn = _dot(ckvn, wkn_ref[...]).astype(_BF16)
    hw = MLA_HEADS * LANES
    for h in range(MLA_HEADS):
        sl = slice(h * LANES, (h + 1) * LANES)
        qn = p2t[sl] * (MLA_SCALE * LOG2_E)
        qr = (p2t[hw + h * LANES:hw + (h + 1) * LANES] * cost
              + p2t[2 * hw + h * LANES:2 * hw + (h + 1) * LANES] * sint) * (MLA_SCALE * LOG2_E)
        qtcat_ref[0, h, :LANES, :] = qn.astype(_BF16)
        qtcat_ref[0, h, LANES:, :] = qr.astype(_BF16)
        kcat_ref[0, :, h * MLA_QK_PAD:h * MLA_QK_PAD + LANES] = kn[:, sl]
        kcat_ref[0, :, h * MLA_QK_PAD + LANES:(h + 1) * MLA_QK_PAD] = kr_roped
    vt = _dot_nt(wvmt_ref[...], ckvn).astype(_BF16)
    ts = vt.shape[1]
    ones_row = (lax.broadcasted_iota(jnp.int32, (V_EXT - V_DIM, ts), 0) == 0).astype(_BF16)
    for h in range(MLA_HEADS):
        vmt_ref[0, h, :V_DIM, :] = vt[h * V_DIM:(h + 1) * V_DIM]
        vmt_ref[0, h, V_DIM:, :] = ones_row


def _full(shape):
    return pl.BlockSpec(shape, lambda *_: (0,) * len(shape))


def _proj(x, pos_row, invf, g_attn, w1, wvt, wkrt, g_qa, g_kva, w2t, wkn, wvmt):
    bsz, seq, d = x.shape
    ts = PROJ_TILE
    grid = (bsz, seq // ts)
    row = lambda b, s: (b, s, 0)
    col = lambda b, s: (b, 0, s)
    out_shape = (
        jax.ShapeDtypeStruct((bsz, seq, SB_WIDTH), _BF16),
        jax.ShapeDtypeStruct((bsz, seq, SB_WIDTH), _BF16),
        jax.ShapeDtypeStruct((bsz, SB_WIDTH, seq), _BF16),
        jax.ShapeDtypeStruct((bsz, MLA_HEADS, MLA_QK_PAD, seq), _BF16),
        jax.ShapeDtypeStruct((bsz, seq, MLA_HEADS * MLA_QK_PAD), _BF16),
        jax.ShapeDtypeStruct((bsz, MLA_HEADS, V_EXT, seq), _BF16),
    )
    heads_col = lambda b, s: (b, 0, 0, s)
    return pl.pallas_call(
        _proj_kernel,
        grid=grid,
        in_specs=[
            pl.BlockSpec((1, ts, d), row),
            pl.BlockSpec((1, 1, ts), col),
            _full(invf.shape), _full(g_attn.shape), _full(w1.shape),
            _full(wvt.shape), _full(wkrt.shape), _full(g_qa.shape), _full(g_kva.shape),
            _full(w2t.shape), _full(wkn.shape), _full(wvmt.shape),
        ],
        out_specs=(
            pl.BlockSpec((1, ts, SB_WIDTH), row),
            pl.BlockSpec((1, ts, SB_WIDTH), row),
            pl.BlockSpec((1, SB_WIDTH, ts), col),
            pl.BlockSpec((1, MLA_HEADS, MLA_QK_PAD, ts), heads_col),
            pl.BlockSpec((1, ts, MLA_HEADS * MLA_QK_PAD), row),
            pl.BlockSpec((1, MLA_HEADS, V_EXT, ts), heads_col),
        ),
        out_shape=out_shape,
        compiler_params=pltpu.CompilerParams(
            dimension_semantics=("arbitrary", "arbitrary"), vmem_limit_bytes=VMEM_LIMIT),
        name="proj",
    )(x, pos_row, invf, g_attn, w1, wvt, wkrt, g_qa, g_kva, w2t, wkn, wvmt)


def _sb_parts(q_ref, k_ref, vt_ref, tri_ref, o_ref, acc_ref, mass_ref, minmass_ref, nls_ref,
              spb_ref, ab_ref):
    t = SB_TILE
    tri = tri_ref[...]
    sub = lax.broadcasted_iota(jnp.int32, (LANES, t), 0)
    first = sub < SB_HEAD_DIM

    def strict_mask():
        key = lax.broadcasted_iota(jnp.int32, (t, t), 0)
        qry = lax.broadcasted_iota(jnp.int32, (t, t), 1)
        return key < qry

    def stage_softplus(c, qt, j, masked):
        start = pl.multiple_of(j * t, t)
        z = _dot(k_ref[0, pl.ds(start, t), :], qt)
        if masked:
            z = jnp.where(strict_mask(), z, SB_MASKED_SCORE)
        zb = z.astype(_BF16)
        pos = jnp.maximum(zb, 0.0)
        neg = zb - pos
        lg = jnp.log2((1.0 + jnp.exp2(neg - pos)).astype(_F32)).astype(_BF16)
        sp = pos + lg
        spb_ref[c] = sp
        nls_ref[c] = neg - lg
        return sp[:1, :].astype(_F32)

    def stage_weights(c, sp_row0):
        excl = _dot(tri, spb_ref[c])
        ab_ref[c] = jnp.exp2(nls_ref[c] - excl.astype(_BF16))
        return excl[:1, :] + sp_row0

    def stage_pv(c, j):
        start = pl.multiple_of(j * t, t)
        return _dot(vt_ref[0, :, pl.ds(start, t)], ab_ref[c])

    def load_qt(i):
        start = pl.multiple_of(i * t, t)
        qt = q_ref[0, pl.ds(start, t), :].astype(_F32).T
        zero = jnp.zeros_like(qt)
        return (jnp.where(first, qt, zero).astype(_BF16), jnp.where(first, zero, qt).astype(_BF16))

    def straight_line(i):
        state = {}

        def begin():
            qts = load_qt(i)
            prev = jnp.maximum(i - 1, 0)
            state["chains"] = [(qts[0], i, True), (qts[0], prev, False),
                               (qts[1], i, True), (qts[1], prev, False)]
            state["rows0"], state["tots"] = {}, {}

        def softplus_step(c):
            def step():
                state["rows0"][c] = stage_softplus(c, *state["chains"][c])
            return step

        def weights_step(c):
            def step():
                state["tots"][c] = stage_weights(c, state["rows0"][c])
            return step

        def combine_step(hh):
            def step():
                d = 2 * hh
                tots, chains = state["tots"], state["chains"]
                pv_diag, pv_prev = stage_pv(d, chains[d][1]), stage_pv(d + 1, chains[d + 1][1])
                has_prev = jnp.where(i > 0, 1.0, 0.0).astype(_F32)
                acc = pv_diag + (jnp.exp2(-tots[d]) * has_prev) * pv_prev
                mass = tots[d] + has_prev * tots[d + 1]
                acc_ref[hh] = acc
                mass_ref[hh] = mass
                state["acc", hh] = acc
                minmass_ref[...] = jnp.minimum(
                    minmass_ref[...], jnp.where(i >= 2, mass, jnp.finfo(_F32).max))
                if hh == 1:
                    out_t = jnp.where(first, state["acc", 0], acc)
                    o_ref[0, pl.ds(pl.multiple_of(i * t, t), t), :] = out_t.T
            return step

        return dict(begin=begin, softplus=[softplus_step(c) for c in range(4)],
                    weights=[weights_step(c) for c in range(4)],
                    combine=[combine_step(hh) for hh in range(2)])

    def sweep_rest(i):
        def min_mass():
            return jnp.min(jnp.minimum(mass_ref[0], mass_ref[1]))

        def cond(c):
            j, mm = c
            return jnp.logical_and(j >= 0, mm < SB_DEAD_MASS)

        def body(c):
            j, _ = c
            qts = load_qt(i)
            for hh in range(2):
                tot = stage_weights(hh, stage_softplus(hh, qts[hh], j, False))
                mass = mass_ref[hh]
                acc_ref[hh] += jnp.exp2(-mass) * stage_pv(hh, j)
                mass_ref[hh] = mass + tot
            return j - 1, min_mass()

        lax.while_loop(cond, body, (i - 2, min_mass()))
        out_t = jnp.where(first, acc_ref[0], acc_ref[1])
        o_ref[0, pl.ds(pl.multiple_of(i * t, t), t), :] = out_t.T

    def finish_all(n_tiles):
        @pl.when(jnp.min(minmass_ref[...]) < SB_DEAD_MASS)
        def _():
            def one(i, carry):
                sb = straight_line(i)
                for step in [sb["begin"]] + sb["softplus"] + sb["weights"] + sb["combine"]:
                    step()
                sweep_rest(i)
                return carry
            lax.fori_loop(0, n_tiles, one, 0)

    return straight_line, finish_all


def _mla_parts(qt_ref, k_ref, vt_ref, bias_ref, o_ref, acc_ref, m_ref):
    t = MLA_TILE
    n_q = k_ref.shape[1] // t

    def scores_into(dst_ref, i, j):
        i = jnp.minimum(i, n_q - 1)
        j = jnp.minimum(j, n_q - 1)
        qs = pl.multiple_of(i * t, t)
        ks = pl.multiple_of(j * t, t)
        dst_ref[...] = _dot(k_ref[0, pl.ds(ks, t), :], qt_ref[0, 0, :, pl.ds(qs, t)])

    def update_half(src_ref, i, j, diagonal, half):
        w = t // 2
        lanes = slice(half * w, (half + 1) * w)
        ks = pl.multiple_of(j * t, t)
        vt = vt_ref[0, 0, :, pl.ds(ks, t)]
        s = src_ref[:, lanes]
        if diagonal:
            s = s + bias_ref[:, lanes]
        m_old = m_ref[i, :, lanes]
        m_new = jnp.maximum(m_old, jnp.max(s, axis=0, keepdims=True))
        p = jnp.exp2(s - m_new).astype(_BF16)
        alpha = jnp.exp2(m_old - m_new)
        acc = alpha * acc_ref[i, :, lanes] + _dot(vt, p)
        if diagonal:
            qs = pl.multiple_of(i * t + half * w, w)
            o_ref[0, pl.ds(qs, w), :] = (acc[:V_DIM, :] / acc[V_DIM:V_DIM + 1, :]).T
        else:
            acc_ref[i, :, lanes] = acc
            m_ref[i, :, lanes] = m_new

    return scores_into, update_half


def _attn_kernel(qt_ref, km_ref, vtm_ref, bias_ref, qs_ref, ks_ref, vts_ref, tri_ref,
                 om_ref, os_ref, acc_ref, m_ref, sa_ref, sb_ref, sacc_ref, smass_ref, minmass_ref,
                 nls_ref, spb_ref, ab_ref):
    scores_into, update_half = _mla_parts(qt_ref, km_ref, vtm_ref, bias_ref, om_ref, acc_ref, m_ref)
    sb_straight, sb_finish_all = _sb_parts(qs_ref, ks_ref, vts_ref, tri_ref, os_ref, sacc_ref,
                                           smass_ref, minmass_ref, nls_ref, spb_ref, ab_ref)
    n_q = km_ref.shape[1] // MLA_TILE
    n_off = n_q * (n_q - 1) // 2
    n_sb = qs_ref.shape[1] // SB_TILE

    acc_ref[...] = jnp.zeros_like(acc_ref)
    m_ref[...] = jnp.full_like(m_ref, jnp.finfo(_F32).min)
    minmass_ref[...] = jnp.full_like(minmass_ref, jnp.finfo(_F32).max)

    def advance(i, j):
        wrap = j + 1 >= i
        return jnp.where(wrap, i + 1, i), jnp.where(wrap, 0, j + 1)

    def tile_steps(src_ref, i, j, diagonal):
        return [lambda: update_half(src_ref, i, j, diagonal, 0),
                lambda: update_half(src_ref, i, j, diagonal, 1)]

    def update(src_ref, i, j, diagonal):
        for step in tile_steps(src_ref, i, j, diagonal):
            step()

    def off_steps(n, carry):
        i, j = carry
        i1, j1 = advance(i, j)
        i2, j2 = advance(i1, j1)
        return ([lambda: scores_into(sb_ref, i1, j1)] + tile_steps(sa_ref, i, j, False)
                + [lambda: scores_into(sa_ref, i2, j2)]
                + tile_steps(sb_ref, i1, j1, False)), (i2, j2)

    def diag_steps(n, carry):
        i = 2 * n
        return ([lambda: scores_into(sb_ref, i + 1, i + 1)] + tile_steps(sa_ref, i, i, True)
                + [lambda: scores_into(sa_ref, i + 2, i + 2)]
                + tile_steps(sb_ref, i + 1, i + 1, True)), carry

    def plain(mla_steps):
        def body(n, carry):
            steps, carry = mla_steps(n, carry)
            for step in steps:
                step()
            return carry
        return body

    def with_sb(first_sb_tile, mla_steps):
        def body(n, carry):
            mla, carry = mla_steps(n, carry)
            sb = sb_straight(first_sb_tile + n)
            sp, wt, cb = sb["softplus"], sb["weights"], sb["combine"]
            order = [sb["begin"], mla[0], sp[0], mla[1], mla[2], sp[1], wt[0], sp[2], wt[1], mla[3],
                     sp[3], cb[0], wt[2], mla[4], mla[5], wt[3], cb[1]]
            for step in order:
                step()
            return carry
        return body

    sb_done = 0
    trips = n_off // 2
    if n_off:
        scores_into(sa_ref, 1, 0)
        fused = min(trips, n_sb - sb_done)
        carry = (jnp.int32(1), jnp.int32(0))
        carry = lax.fori_loop(0, fused, with_sb(sb_done, off_steps), carry)
        carry = lax.fori_loop(fused, trips, plain(off_steps), carry)
        sb_done += fused
        if n_off % 2:
            update(sa_ref, carry[0], carry[1], False)

    trips = n_q // 2
    scores_into(sa_ref, 0, 0)
    fused = min(trips, n_sb - sb_done)
    lax.fori_loop(0, fused, with_sb(sb_done, diag_steps), 0)
    lax.fori_loop(fused, trips, plain(diag_steps), 0)
    sb_done += fused
    if n_q % 2:
        update(sa_ref, n_q - 1, n_q - 1, True)

    def sb_only(n, carry):
        sb = sb_straight(n)
        for step in [sb["begin"]] + sb["softplus"] + sb["weights"] + sb["combine"]:
            step()
        return carry

    lax.fori_loop(sb_done, n_sb, sb_only, 0)
    sb_finish_all(n_sb)


def _attention(qtcat, kcat, vmt, bias, qsb, ksb, vtsb, tri):
    bsz, seq, _ = kcat.shape
    assert MLA_HEADS == SB_WIDTH // LANES
    t, ts = MLA_TILE, SB_TILE
    head_rows = lambda b, h: (b, h, 0, 0)
    lane_block = lambda b, h: (b, 0, h)
    return pl.pallas_call(
        _attn_kernel,
        grid=(bsz, MLA_HEADS),
        in_specs=[
            pl.BlockSpec((1, 1, MLA_QK_PAD, seq), head_rows),
            pl.BlockSpec((1, seq, MLA_QK_PAD), lane_block),
            pl.BlockSpec((1, 1, V_EXT, seq), head_rows),
            pl.BlockSpec((t, t), lambda b, h: (0, 0)),
            pl.BlockSpec((1, seq, LANES), lane_block),
            pl.BlockSpec((1, seq, LANES), lane_block),
            pl.BlockSpec((1, LANES, seq), lambda b, h: (b, h, 0)),
            pl.BlockSpec((ts, ts), lambda b, h: (0, 0)),
        ],
        out_specs=(pl.BlockSpec((1, seq, V_DIM), lane_block),
                   pl.BlockSpec((1, seq, LANES), lane_block)),
        out_shape=(jax.ShapeDtypeStruct((bsz, seq, MLA_WIDTH), _F32),
                   jax.ShapeDtypeStruct((bsz, seq, SB_WIDTH), _F32)),
        scratch_shapes=[pltpu.VMEM((seq // t, V_EXT, t), _F32), pltpu.VMEM((seq // t, 1, t), _F32),
                        pltpu.VMEM((t, t), _F32), pltpu.VMEM((t, t), _F32),
                        pltpu.VMEM((2, LANES, ts), _F32), pltpu.VMEM((2, 1, ts), _F32),
                        pltpu.VMEM((1, ts), _F32),
                        pltpu.VMEM((4, ts, ts), _BF16), pltpu.VMEM((4, ts, ts), _BF16),
                        pltpu.VMEM((4, ts, ts), _BF16)],
        compiler_params=pltpu.CompilerParams(
            dimension_semantics=("arbitrary", "arbitrary"), vmem_limit_bytes=VMEM_LIMIT),
        name="attention",
    )(qtcat, kcat, vmt, bias, qsb, ksb, vtsb, tri)


def _post_kernel(x_ref, osb_ref, omla_ref, g_sb_ref, g_mla_ref, wo_ref, g_mlp_ref, wup_ref,
                 wdown_ref, g_out_ref, out_ref, *, apply_out_norm):
    ms = _rms(osb_ref[...], g_sb_ref[...]).astype(_BF16)
    mm = _rms(omla_ref[...], g_mla_ref[...]).astype(_BF16)
    h1 = x_ref[...] + _dot(jnp.concatenate([ms, mm], axis=1), wo_ref[...])
    v = _rms(h1, g_mlp_ref[...]).astype(_BF16)
    hid = jnp.square(jnp.maximum(_dot(v, wup_ref[...]), 0.0)).astype(_BF16)
    h2 = h1 + _dot(hid, wdown_ref[...])
    out_ref[...] = _rms(h2, g_out_ref[...]) if apply_out_norm else h2


def _post(x2, osb2, omla2, g_sb, g_mla, wo, g_mlp, wup, wdown, g_out, apply_out_norm):
    n, d = x2.shape
    tm = POST_TILE
    once = pl.Buffered(1)
    rows = lambda w: pl.BlockSpec((tm, w), lambda r: (r, 0))
    const = lambda a: pl.BlockSpec(a.shape, lambda r: (0, 0), pipeline_mode=once)
    return pl.pallas_call(
        functools.partial(_post_kernel, apply_out_norm=apply_out_norm),
        grid=(n // tm,),
        in_specs=[rows(d), rows(SB_WIDTH), rows(MLA_WIDTH), const(g_sb), const(g_mla), const(wo),
                  const(g_mlp), const(wup), const(wdown), const(g_out)],
        out_specs=rows(d),
        out_shape=jax.ShapeDtypeStruct((n, d), _F32),
        compiler_params=pltpu.CompilerParams(
            dimension_semantics=("arbitrary",), vmem_limit_bytes=VMEM_LIMIT),
        name="post",
    )(x2, osb2, omla2, g_sb, g_mla, wo, g_mlp, wup, wdown, g_out)


def _swap_halves(w):
    half = QK_ROPE // 2
    return jnp.concatenate([w[..., half:], w[..., :half]], axis=-1)


def _layer_weights(w_in, w_q_b, w_kv_b):
    d = w_in.shape[0]
    s2, s3 = 2 * SB_WIDTH, 3 * SB_WIDTH
    s5 = s3 + Q_LORA + KV_LORA
    w1 = jnp.concatenate([w_in[:, :s2], w_in[:, s3:s5]], axis=1).astype(_BF16)
    wvt = w_in[:, s2:s3].T.astype(_BF16)
    wkr = w_in[:, s5:]
    pad = jnp.zeros((d, LANES - QK_ROPE), w_in.dtype)
    wkrt = jnp.concatenate([wkr, pad, _swap_halves(wkr), pad], axis=1).T.astype(_BF16)
    wqb = w_q_b.reshape(Q_LORA, MLA_HEADS, QK_NOPE + QK_ROPE)
    qn = wqb[:, :, :QK_NOPE].reshape(Q_LORA, MLA_HEADS * QK_NOPE)
    qr = wqb[:, :, QK_NOPE:]
    qpad = jnp.zeros((Q_LORA, MLA_HEADS, LANES - QK_ROPE), w_q_b.dtype)
    qr_pad = jnp.concatenate([qr, qpad], axis=-1).reshape(Q_LORA, MLA_HEADS * LANES)
    qrr_pad = jnp.concatenate([_swap_halves(qr), qpad], axis=-1).reshape(Q_LORA, MLA_HEADS * LANES)
    w2t = jnp.concatenate([qn, qr_pad, qrr_pad], axis=1).T.astype(_BF16)
    wkvb = w_kv_b.reshape(KV_LORA, MLA_HEADS, QK_NOPE + V_DIM)
    wkn = wkvb[:, :, :QK_NOPE].reshape(KV_LORA, MLA_HEADS * QK_NOPE).astype(_BF16)
    wvmt = wkvb[:, :, QK_NOPE:].reshape(KV_LORA, MLA_HEADS * V_DIM).T.astype(_BF16)
    return w1, wvt, wkrt, w2t, wkn, wvmt


def kernel(x, positions, attn_norm_g, w_in, q_a_norm_g, w_q_b, kv_a_norm_g, w_kv_b, sb_out_norm_g,
           mla_out_norm_g, w_o, mlp_norm_g, w_up, w_down, final_norm_g):
    bsz, seq, d = x.shape
    depth = w_in.shape[0]
    assert seq % PROJ_TILE == 0 and seq % SB_TILE == 0 and seq % MLA_TILE == 0
    assert (bsz * seq) % POST_TILE == 0

    half = QK_ROPE // 2
    inv_freq = ROPE_BASE ** (-jnp.arange(half, dtype=_F32) / half)
    invf = inv_freq[:, None]
    pos_row = positions[:, None, :]
    r = lax.broadcasted_iota(jnp.int32, (SB_TILE, SB_TILE), 0)
    c = lax.broadcasted_iota(jnp.int32, (SB_TILE, SB_TILE), 1)
    tri = (c > r).astype(_BF16)
    key = lax.broadcasted_iota(jnp.int32, (MLA_TILE, MLA_TILE), 0)
    qry = lax.broadcasted_iota(jnp.int32, (MLA_TILE, MLA_TILE), 1)
    bias = jnp.where(key <= qry, 0.0, MLA_MASK_BIAS).astype(_F32)

    row = lambda g: g[None, :]
    h = x
    for l in range(depth):
        w1, wvt, wkrt, w2t, wkn, wvmt = _layer_weights(w_in[l], w_q_b[l], w_kv_b[l])
        qsb, ksb, vtsb, qtcat, kcat, vmt = _proj(
            h, pos_row, invf, row(attn_norm_g[l]), w1, wvt, wkrt, row(q_a_norm_g[l]),
            row(kv_a_norm_g[l]), w2t, wkn, wvmt)
        omla, osb = _attention(qtcat, kcat, vmt, bias, qsb, ksb, vtsb, tri)
        last = l == depth - 1
        h = _post(h.reshape(bsz * seq, d), osb.reshape(bsz * seq, SB_WIDTH),
                  omla.reshape(bsz * seq, MLA_WIDTH), row(sb_out_norm_g[l]),
                  row(mla_out_norm_g[l]), w_o[l].astype(_BF16), row(mlp_norm_g[l]),
                  w_up[l].astype(_BF16), w_down[l].astype(_BF16), row(final_norm_g),
                  last).reshape(bsz, seq, d)
    return h
```
